```python
import functools
import jax, jax.numpy as jnp
from jax import lax
import numpy as np

D_MODEL = 2048
BATCH = 8
SEQ = 2048
DEPTH = 2
DEC_BATCH = 32
DEC_SEQ = 1
PAST_LEN = 8192
PAGE_SIZE = 128

N_META = 16
N_HEADS = 8
N_KV_HEADS = 2
HEAD_DIM = 128
ATTN_WIDTH = N_HEADS * HEAD_DIM
KV_WIDTH = N_KV_HEADS * HEAD_DIM
N_IDX_HEADS = 16
IDX_DIM = 64
TOPK_MAX = 256
POOL_WINDOWS = (2, 4, 8, 16)
POOL_WIDTH = D_MODEL - ATTN_WIDTH
POOL_GROUP = POOL_WIDTH // len(POOL_WINDOWS)
POOL_BUF = max(POOL_WINDOWS) - 1
MIX_WIDTH = ATTN_WIDTH + POOL_WIDTH
D_FF = 5632
CONV_WIDTH = 3
Q_BLOCK = 128
RMS_EPS = 1e-6

Q_END = ATTN_WIDTH
K_END = Q_END + KV_WIDTH
V_END = K_END + KV_WIDTH
IQ_END = V_END + N_IDX_HEADS * IDX_DIM
IK_END = IQ_END + IDX_DIM
IW_END = IK_END + N_IDX_HEADS
IN_COLS = IW_END + POOL_WIDTH
IN_OFFSETS = (Q_END, K_END, V_END, IQ_END, IK_END, IW_END)

kernel_name = "hymba_dsa_pool_convffn_step"


def _rmsnorm(x, g):
    xf = x.astype(jnp.float32)
    y = xf * lax.rsqrt(jnp.mean(xf * xf, axis=-1, keepdims=True) + RMS_EPS)
    return (y * g.astype(jnp.float32)).astype(x.dtype)


def _index_scores(iq, iw, ik):
    s = jnp.einsum('bqhd,bsd->bqsh', iq, ik).astype(jnp.float32) * (IDX_DIM ** -0.5)
    w = iw.astype(jnp.float32) * (N_IDX_HEADS ** -0.5)
    return jnp.einsum('bqsh,bqh->bqs', jax.nn.relu(s), w)


def _select_keys(scores, qpos, kpos, topk):
    admissible = kpos[None, None, :] <= qpos[None, :, None]
    scores = jnp.where(admissible, scores, -jnp.inf)
    _, idx = lax.top_k(scores, topk)
    valid = idx <= qpos[None, :, None]
    return idx, valid


def _attend_selected(q, k_sel, v_sel, valid):
    B, Q = q.shape[:2]
    qg = q.reshape(B, Q, N_KV_HEADS, N_HEADS // N_KV_HEADS, HEAD_DIM)
    s = jnp.einsum('bqgrd,bqkgd->bqgrk', qg, k_sel).astype(jnp.float32) * (HEAD_DIM ** -0.5)
    s = jnp.where(valid[:, :, None, None, :], s, -jnp.inf)
    p = jax.nn.softmax(s, axis=-1)
    o = jnp.einsum('bqgrk,bqkgd->bqgrd', p.astype(v_sel.dtype), v_sel)
    return o.reshape(B, Q, N_HEADS, HEAD_DIM)


def _to_blocks(a, n_blk):
    pad = n_blk * Q_BLOCK - a.shape[1]
    a = jnp.pad(a, [(0, 0), (0, pad)] + [(0, 0)] * (a.ndim - 2))
    return jnp.moveaxis(a.reshape((a.shape[0], n_blk, Q_BLOCK) + a.shape[2:]), 1, 0)


def _sparse_attn_prompt(q, k, v, iq, ik, iw, topk):
    B, T = q.shape[:2]
    n_blk = -(-T // Q_BLOCK)
    kpos = jnp.arange(T)
    gather = jax.vmap(lambda rows, ii: rows[ii])

    def block(args):
        qb, iqb, iwb, blk = args
        qpos = blk * Q_BLOCK + jnp.arange(Q_BLOCK)
        idx, valid = _select_keys(_index_scores(iqb, iwb, ik), qpos, kpos, topk)
        return _attend_selected(qb, gather(k, idx), gather(v, idx), valid)

    out = lax.map(block, (_to_blocks(q, n_blk), _to_blocks(iq, n_blk),
                          _to_blocks(iw, n_blk), jnp.arange(n_blk)))
    out = jnp.moveaxis(out, 0, 1).reshape(B, n_blk * Q_BLOCK, N_HEADS, HEAD_DIM)
    return out[:, :T]


def _sparse_attn_sample(q, k_new, v_new, iq, ik_new, iw, cache_k_l, cache_v_l, cache_kidx_l, page_table, topk):
    DB, DS = q.shape[:2]
    past = page_table.shape[1] * PAGE_SIZE
    ik_past = cache_kidx_l[page_table].reshape(DB, past, IDX_DIM)
    ik_all = jnp.concatenate([ik_past, ik_new.astype(ik_past.dtype)], axis=1)
    qpos = past + jnp.arange(DS)
    kpos = jnp.arange(past + DS)
    idx, valid = _select_keys(_index_scores(iq, iw, ik_all), qpos, kpos, topk)
    pidx = jnp.minimum(idx, past - 1)
    phys = jax.vmap(lambda pt, ii: pt[ii])(page_table, pidx // PAGE_SIZE)
    slot = pidx % PAGE_SIZE
    nidx = jnp.clip(idx - past, 0, DS - 1)
    gather = jax.vmap(lambda rows, ii: rows[ii])
    is_new = (idx >= past)[..., None, None]
    k_sel = jnp.where(is_new, gather(k_new, nidx), cache_k_l[phys, slot].astype(k_new.dtype))
    v_sel = jnp.where(is_new, gather(v_new, nidx), cache_v_l[phys, slot].astype(v_new.dtype))
    return _attend_selected(q, k_sel, v_sel, valid)


def _pool_mixer(u, buf, pos0, pool_w, pool_scale):
    T = u.shape[1]
    hp = jnp.concatenate([buf.astype(u.dtype), u], axis=1)
    upf = hp.astype(jnp.float32)
    csum = jnp.concatenate([jnp.zeros_like(upf[:, :1]), jnp.cumsum(upf, axis=1)], axis=1)
    pos = pos0 + jnp.arange(T)
    outs = []
    for g, w in enumerate(POOL_WINDOWS):
        cg = csum[..., g * POOL_GROUP:(g + 1) * POOL_GROUP]
        wsum = cg[:, POOL_BUF + 1:] - cg[:, POOL_BUF + 1 - w:POOL_BUF + 1 - w + T]
        cnt = jnp.minimum(pos + 1, w).astype(jnp.float32)[None, :, None]
        ug = upf[:, POOL_BUF:, g * POOL_GROUP:(g + 1) * POOL_GROUP]
        d = (wsum / cnt - ug).astype(u.dtype)
        outs.append(jnp.einsum('btc,cd->btd', d, pool_w[g]))
    y = jnp.concatenate(outs, axis=-1) * pool_scale
    return y, hp[:, -POOL_BUF:]


def _conv_ffn(x, buf, w_up, conv_w, conv_b, w_down):
    T = x.shape[1]
    h = jnp.einsum('btd,df->btf', x, w_up)
    hp = jnp.concatenate([buf.astype(h.dtype), h], axis=1)
    c = conv_b + conv_w[0] * hp[:, 0:T]
    for i in range(1, CONV_WIDTH):
        c = c + conv_w[i] * hp[:, i:i + T]
    gate, val = jnp.split(c, 2, axis=-1)
    y = jnp.einsum('btf,fd->btd', jax.nn.silu(gate) * val, w_down)
    return y, hp[:, -(CONV_WIDTH - 1):]


def _layer(x, attn_fn, pool_buf, conv_buf, pos0, w_in, w_o, pool_w, pool_scale,
           g_mix_pre, g_mix_post, g_ffn_pre, g_ffn_post, w_up, conv_w, conv_b, w_down):
    B, T, _ = x.shape
    h = _rmsnorm(x, g_mix_pre)
    proj = jnp.einsum('btd,dc->btc', h, w_in)
    q, k, v, iq, ik, iw, u = jnp.split(proj, IN_OFFSETS, axis=-1)
    q = q.reshape(B, T, N_HEADS, HEAD_DIM)
    k = k.reshape(B, T, N_KV_HEADS, HEAD_DIM)
    v = v.reshape(B, T, N_KV_HEADS, HEAD_DIM)
    iq = iq.reshape(B, T, N_IDX_HEADS, IDX_DIM)
    a = attn_fn(q, k, v, iq, ik, iw).reshape(B, T, ATTN_WIDTH)
    p, new_pool = _pool_mixer(u, pool_buf, pos0, pool_w, pool_scale)
    mix = jnp.concatenate([a, p.astype(a.dtype)], axis=-1)
    x = x + _rmsnorm(jnp.einsum('btm,md->btd', mix, w_o), g_mix_post)
    f, new_conv = _conv_ffn(_rmsnorm(x, g_ffn_pre), conv_buf, w_up, conv_w, conv_b, w_down)
    x = x + _rmsnorm(f, g_ffn_post)
    return x, k, v, ik, new_pool, new_conv


def setup_inputs(seed: int = 0) -> dict:
    key = jax.random.key(seed)
    ks = jax.random.split(key, 24)
    f32 = jnp.float32
    n_pages = PAST_LEN // PAGE_SIZE
    n_used = DEC_BATCH * n_pages
    n_pool = n_used + max(1, n_used // 4)
    nrm = lambda k, shape, s: jax.random.normal(k, shape, f32) * s
    page_table = jax.random.permutation(ks[0], n_pool)[:n_used].reshape(DEC_BATCH, n_pages).astype(jnp.int32)
    return {
        "x_prompt": nrm(ks[1], (BATCH, SEQ, D_MODEL), 1.0),
        "x_sample": nrm(ks[2], (DEC_BATCH, DEC_SEQ, D_MODEL), 1.0),
        "cache_k": nrm(ks[3], (DEPTH, n_pool, PAGE_SIZE, N_KV_HEADS, HEAD_DIM), 1.0),
        "cache_v": nrm(ks[4], (DEPTH, n_pool, PAGE_SIZE, N_KV_HEADS, HEAD_DIM), 1.0),
        "cache_kidx": nrm(ks[5], (DEPTH, n_pool, PAGE_SIZE, IDX_DIM), 1.0),
        "state_pool": nrm(ks[6], (DEPTH, DEC_BATCH, POOL_BUF, POOL_WIDTH), 1.0),
        "state_conv": nrm(ks[7], (DEPTH, DEC_BATCH, CONV_WIDTH - 1, 2 * D_FF), 1.0),
        "page_table": page_table,
        "meta_tokens": nrm(ks[8], (N_META, D_MODEL), 1.0),
        "w_in": nrm(ks[9], (DEPTH, D_MODEL, IN_COLS), D_MODEL ** -0.5),
        "w_o": nrm(ks[10], (DEPTH, MIX_WIDTH, D_MODEL), MIX_WIDTH ** -0.5),
        "pool_w": nrm(ks[11], (DEPTH, len(POOL_WINDOWS), POOL_GROUP, POOL_GROUP), POOL_GROUP ** -0.5),
        "pool_scale": 1.0 + nrm(ks[12], (DEPTH, POOL_WIDTH), 0.1),
        "g_mix_pre": 1.0 + nrm(ks[13], (DEPTH, D_MODEL), 0.02),
        "g_mix_post": 1.0 + nrm(ks[14], (DEPTH, D_MODEL), 0.02),
        "g_ffn_pre": 1.0 + nrm(ks[15], (DEPTH, D_MODEL), 0.02),
        "g_ffn_post": 1.0 + nrm(ks[16], (DEPTH, D_MODEL), 0.02),
        "w_up": nrm(ks[17], (DEPTH, D_MODEL, 2 * D_FF), D_MODEL ** -0.5),
        "conv_w": nrm(ks[18], (DEPTH, CONV_WIDTH, 2 * D_FF), CONV_WIDTH ** -0.5),
        "conv_b": nrm(ks[19], (DEPTH, 2 * D_FF), 0.01),
        "w_down": nrm(ks[20], (DEPTH, D_FF, D_MODEL), D_FF ** -0.5),
    }


def reference(x_prompt, x_sample, cache_k, cache_v, cache_kidx, state_pool, state_conv, page_table,
              meta_tokens, w_in, w_o, pool_w, pool_scale, g_mix_pre, g_mix_post, g_ffn_pre, g_ffn_post,
              w_up, conv_w, conv_b, w_down):
    B = x_prompt.shape[0]
    DS = x_sample.shape[1]
    meta = jnp.broadcast_to(meta_tokens[None].astype(x_prompt.dtype), (B, N_META, D_MODEL))
    xp = jnp.concatenate([meta, x_prompt], axis=1)
    xs = x_sample
    T_p = xp.shape[1]
    past = page_table.shape[1] * PAGE_SIZE
    topk_p = min(TOPK_MAX, T_p // 4)
    topk_s = min(TOPK_MAX, (past + DS) // 4)
    pool_buf0 = jnp.zeros((B, POOL_BUF, POOL_WIDTH), x_prompt.dtype)
    conv_buf0 = jnp.zeros((B, CONV_WIDTH - 1, 2 * D_FF), x_prompt.dtype)
    kp, vp, ikp, poolp, convp = [], [], [], [], []
    ks_, vs_, iks_, pools_, convs_ = [], [], [], [], []
    for l in range(DEPTH):
        wl = (w_in[l], w_o[l], pool_w[l], pool_scale[l], g_mix_pre[l], g_mix_post[l],
              g_ffn_pre[l], g_ffn_post[l], w_up[l], conv_w[l], conv_b[l], w_down[l])
        attn_p = functools.partial(_sparse_attn_prompt, topk=topk_p)
        xp, k1, v1, ik1, pl1, cv1 = _layer(xp, attn_p, pool_buf0, conv_buf0, 0, *wl)
        attn_s = functools.partial(_sparse_attn_sample, cache_k_l=cache_k[l], cache_v_l=cache_v[l],
                                   cache_kidx_l=cache_kidx[l], page_table=page_table, topk=topk_s)
        xs, k2, v2, ik2, pl2, cv2 = _layer(xs, attn_s, state_pool[l], state_conv[l], past, *wl)
        kp.append(k1); vp.append(v1); ikp.append(ik1); poolp.append(pl1); convp.append(cv1)
        ks_.append(k2); vs_.append(v2); iks_.append(ik2); pools_.append(pl2); convs_.append(cv2)
    y_prompt = xp[:, N_META:]
    y_sample = xs
    return (y_prompt, y_sample,
            jnp.stack(kp), jnp.stack(vp), jnp.stack(ikp), jnp.stack(poolp), jnp.stack(convp),
            jnp.stack(ks_), jnp.stack(vs_), jnp.stack(iks_), jnp.stack(pools_), jnp.stack(convs_))
```

```python
import functools

import jax
import jax.numpy as jnp
from jax import lax
from jax.experimental import pallas as pl
from jax.experimental.pallas import tpu as pltpu

N_META = 16
N_HEADS = 8
N_KV_HEADS = 2
HEAD_DIM = 128
N_IDX_HEADS = 16
IDX_DIM = 64
TOPK_MAX = 256
POOL_WINDOWS = (2, 4, 8, 16)
CONV_WIDTH = 3
PAGE_SIZE = 128
Q_BLOCK = 128
RMS_EPS = 1e-6

ATTN_WIDTH = N_HEADS * HEAD_DIM
KV_WIDTH = N_KV_HEADS * HEAD_DIM
HEADS_PER_KV = N_HEADS // N_KV_HEADS
POOL_BUF = max(POOL_WINDOWS) - 1

V7X_LANES = 128
V7X_SUBLANES_F32 = 8
V7X_SUBLANES_BF16 = 16
V7X_VMEM_LIMIT_BYTES = 56 * 1024 * 1024

INT_MIN = -(2 ** 31)
F32 = jnp.float32
BF16 = jnp.bfloat16

CONV_HALO = V7X_SUBLANES_F32
POOL_HALO = 2 * V7X_SUBLANES_F32
PAGES_PER_STEP = 8
TM_PROJ_CAP = 384
TM_FFN_CAP = 688


def _params(*sem):
    return pltpu.CompilerParams(dimension_semantics=sem, vmem_limit_bytes=V7X_VMEM_LIMIT_BYTES)


def _row_tile(seq_len, cap):
    best = None
    for d in range(V7X_SUBLANES_BF16, min(seq_len, cap) + 1, V7X_SUBLANES_BF16):
        if seq_len % d == 0:
            best = d
    assert best is not None, seq_len
    return best


def _ff_tile(d_ff, cap=512):
    best = None
    for d in range(V7X_LANES, min(d_ff, cap) + 1, V7X_LANES):
        if d_ff % d == 0:
            best = d
    assert best is not None, d_ff
    return best


def _rms(x, g):
    return x * lax.rsqrt(jnp.mean(x * x, axis=-1, keepdims=True) + RMS_EPS) * g


def _dot(a, b):
    return jnp.dot(a, b, preferred_element_type=F32)


def _dot_t(a, b):
    return lax.dot_general(a, b, (((1,), (1,)), ((), ())), preferred_element_type=F32)


def _const_spec(shape):
    n = len(shape)
    return pl.BlockSpec(shape, lambda *_: (0,) * n)


def _in_proj_kernel(x_ref, g_ref, wq_ref, wkv_ref, wiq_ref, widx_ref, wu_ref,
                    q_ref, k_ref, v_ref, kb_ref, vb_ref, iq_ref, ik_ref, ik2_ref, iw_ref, u_ref):
    hn = _rms(x_ref[...], g_ref[...]).astype(BF16)
    q_ref[...] = _dot(hn, wq_ref[...]).astype(BF16)
    kv = _dot(hn, wkv_ref[...])
    k_ref[...] = kv[:, :KV_WIDTH]
    v_ref[...] = kv[:, KV_WIDTH:]
    kb_ref[...] = kv[:, :KV_WIDTH].astype(BF16)
    vb_ref[...] = kv[:, KV_WIDTH:].astype(BF16)
    iq_ref[...] = _dot(hn, wiq_ref[...]).astype(BF16)
    idx = _dot(hn, widx_ref[...])
    ik_ref[...] = idx[:, :IDX_DIM]
    ik2_ref[...] = idx[:, :2 * V7X_LANES].astype(BF16)
    iw_ref[...] = idx[:, 2 * V7X_LANES:2 * V7X_LANES + N_IDX_HEADS]
    u_ref[...] = _dot(hn, wu_ref[...])


def _in_proj(x, g, w, tm):
    m, d = x.shape
    pool_width = w["u"].shape[1]
    row = lambda c: pl.BlockSpec((tm, c), lambda i: (i, 0))
    outs = [
        (ATTN_WIDTH, BF16), (KV_WIDTH, F32), (KV_WIDTH, F32), (KV_WIDTH, BF16), (KV_WIDTH, BF16),
        (N_IDX_HEADS * IDX_DIM, BF16), (IDX_DIM, F32), (2 * V7X_LANES, BF16), (N_IDX_HEADS, F32),
        (pool_width, F32),
    ]
    return pl.pallas_call(
        _in_proj_kernel,
        grid=(m // tm,),
        in_specs=[row(d), _const_spec((1, d)), _const_spec(w["q"].shape), _const_spec(w["kv"].shape),
                  _const_spec(w["iq"].shape), _const_spec(w["idx"].shape), _const_spec(w["u"].shape)],
        out_specs=[row(c) for c, _ in outs],
        out_shape=[jax.ShapeDtypeStruct((m, c), dt) for c, dt in outs],
        compiler_params=_params("arbitrary"),
        name="in_proj",
    )(x, g, w["q"], w["kv"], w["iq"], w["idx"], w["u"])


def _sort_key(x):
    bits = lax.bitcast_convert_type(x, jnp.int32)
    return bits ^ (lax.shift_right_arithmetic(bits, 31) & jnp.int32(0x7FFFFFFF))


def _kth_largest_key(count_ge, rows, k):
    def body(it, t):
        cand = t | lax.shift_left(jnp.int32(1), 31 - it)
        cnt = count_ge(cand ^ jnp.int32(INT_MIN))
        return jnp.where(cnt >= k, cand, t)

    t = lax.fori_loop(0, 32, body, jnp.zeros((rows, 1), jnp.int32))
    return t ^ jnp.int32(INT_MIN)


def _attn_prompt_kernel(q_ref, iq_ref, iw_ref, kb_ref, vb_ref, ik2_ref, o_ref,
                        k_scr, v_scr, ik_scr, key_scr, bias_scr, *, topk):
    i = pl.program_id(1)
    t_len = kb_ref.shape[1]
    tk = k_scr.shape[0]

    @pl.when(i == 0)
    def _():
        k_scr[0:t_len, :] = kb_ref[0]
        v_scr[0:t_len, :] = vb_ref[0]
        ik_scr[0:t_len, :] = ik2_ref[0]
        if tk > t_len:
            k_scr[t_len:tk, :] = jnp.zeros((tk - t_len, k_scr.shape[1]), BF16)
            v_scr[t_len:tk, :] = jnp.zeros((tk - t_len, v_scr.shape[1]), BF16)
            ik_scr[t_len:tk, :] = jnp.zeros((tk - t_len, ik_scr.shape[1]), BF16)

    iw = iw_ref[0] * (IDX_DIM ** -0.5 * N_IDX_HEADS ** -0.5)
    acc = jnp.zeros((Q_BLOCK, tk), F32)
    for h in range(N_IDX_HEADS):
        pair, half = divmod(h, 2)
        lhs = iq_ref[0, :, pair * V7X_LANES:(pair + 1) * V7X_LANES]
        rhs = ik_scr[:, half * V7X_LANES:(half + 1) * V7X_LANES]
        acc = acc + jnp.maximum(_dot_t(lhs, rhs), 0.0) * iw[:, h:h + 1]

    qpos = i * Q_BLOCK + lax.broadcasted_iota(jnp.int32, (Q_BLOCK, 1), 0)
    kpos = lax.broadcasted_iota(jnp.int32, (1, tk), 1)
    key_scr[...] = jnp.where(kpos <= qpos, _sort_key(acc), jnp.int32(INT_MIN))

    def count_ge(t):
        return jnp.sum(jnp.where(key_scr[...] >= t, 1.0, 0.0), axis=1, keepdims=True)

    thr = _kth_largest_key(count_ge, Q_BLOCK, topk)
    thr = jnp.maximum(thr, jnp.int32(INT_MIN + 1))
    bias_scr[...] = jnp.where(key_scr[...] >= thr, 0.0, -jnp.inf)

    scale = HEAD_DIM ** -0.5
    for g in range(N_KV_HEADS):
        h0 = g * HEADS_PER_KV
        qg = jnp.concatenate(
            [q_ref[0, :, (h0 + r) * HEAD_DIM:(h0 + r + 1) * HEAD_DIM] for r in range(HEADS_PER_KV)], axis=0)
        s = _dot_t(qg, k_scr[:, g * HEAD_DIM:(g + 1) * HEAD_DIM])
        s = s.reshape(HEADS_PER_KV, Q_BLOCK, tk) * scale + bias_scr[...][None]
        m = jnp.max(s, axis=-1, keepdims=True)
        p = jnp.exp(s - m)
        l = jnp.sum(p, axis=-1, keepdims=True)
        o = _dot(p.reshape(HEADS_PER_KV * Q_BLOCK, tk).astype(BF16), v_scr[:, g * HEAD_DIM:(g + 1) * HEAD_DIM])
        o = o / l.reshape(HEADS_PER_KV * Q_BLOCK, 1)
        for r in range(HEADS_PER_KV):
            o_ref[0, :, (h0 + r) * HEAD_DIM:(h0 + r + 1) * HEAD_DIM] = (
                o[r * Q_BLOCK:(r + 1) * Q_BLOCK].astype(o_ref.dtype))


def _attn_prompt(q, iq, iw, kb, vb, ik2, topk):
    b, t_len, _ = q.shape
    n_blk = pl.cdiv(t_len, Q_BLOCK)
    tk = n_blk * Q_BLOCK
    qblk = lambda c: pl.BlockSpec((1, Q_BLOCK, c), lambda bi, i: (bi, i, 0))
    full = lambda c: pl.BlockSpec((1, t_len, c), lambda bi, i: (bi, 0, 0))
    return pl.pallas_call(
        functools.partial(_attn_prompt_kernel, topk=topk),
        grid=(b, n_blk),
        in_specs=[qblk(ATTN_WIDTH), qblk(N_IDX_HEADS * IDX_DIM), qblk(N_IDX_HEADS),
                  full(KV_WIDTH), full(KV_WIDTH), full(2 * V7X_LANES)],
        out_specs=qblk(ATTN_WIDTH),
        out_shape=jax.ShapeDtypeStruct((b, t_len, ATTN_WIDTH), BF16),
        scratch_shapes=[pltpu.VMEM((tk, KV_WIDTH), BF16), pltpu.VMEM((tk, KV_WIDTH), BF16),
                        pltpu.VMEM((tk, 2 * V7X_LANES), BF16),
                        pltpu.VMEM((Q_BLOCK, tk), jnp.int32), pltpu.VMEM((Q_BLOCK, tk), F32)],
        compiler_params=_params("arbitrary", "arbitrary"),
        name="attn_prompt",
    )(q, iq, iw, kb, vb, ik2)


def _pool_project(d_parts, pw_ref, ps_ref):
    group = d_parts[0].shape[1]
    outs = []
    for g, d in enumerate(d_parts):
        outs.append(_dot(d.astype(BF16), pw_ref[g]) * ps_ref[:, g * group:(g + 1) * group])
    return jnp.concatenate(outs, axis=1).astype(BF16)


def _pos_in_seq(i, tm, seq_len):
    t = (i * tm) % seq_len + lax.broadcasted_iota(jnp.int32, (tm, 1), 0)
    return jnp.where(t >= seq_len, t - seq_len, t)


def _mix_prompt_kernel(x_ref, a_ref, u_ref, wo_ref, pw_ref, ps_ref, g_ref, y_ref, e_scr, *, seq_len):
    i = pl.program_id(0)
    tm = x_ref.shape[0]
    group = pw_ref.shape[1]

    @pl.when(i == 0)
    def _():
        e_scr[0:POOL_HALO, :] = jnp.zeros((POOL_HALO, e_scr.shape[1]), F32)

    u = u_ref[...]
    e_scr[POOL_HALO:POOL_HALO + tm, :] = u
    t = _pos_in_seq(i, tm, seq_len)
    in_seq = [None] + [(t >= s).astype(F32) for s in range(1, max(POOL_WINDOWS))]
    d_parts = []
    for g, w in enumerate(POOL_WINDOWS):
        cols = slice(g * group, (g + 1) * group)
        ug = u[:, cols]
        wsum = ug
        for s in range(1, w):
            wsum = wsum + in_seq[s] * e_scr[POOL_HALO - s:POOL_HALO - s + tm, cols]
        cnt = jnp.minimum(t + 1, w).astype(F32)
        d_parts.append(wsum / cnt - ug)
    e_scr[0:POOL_HALO, :] = u[tm - POOL_HALO:, :]
    p = _pool_project(d_parts, pw_ref, ps_ref)
    o = _dot(a_ref[...], wo_ref[0:ATTN_WIDTH, :]) + _dot(p, wo_ref[ATTN_WIDTH:, :])
    y_ref[...] = x_ref[...] + _rms(o, g_ref[...])


def _mix_prompt(x, a, u, wo, pw, ps, g, tm, seq_len):
    m, d = x.shape
    pool_width = u.shape[1]
    assert POOL_HALO <= tm <= seq_len
    row = lambda c: pl.BlockSpec((tm, c), lambda i: (i, 0))
    return pl.pallas_call(
        functools.partial(_mix_prompt_kernel, seq_len=seq_len),
        grid=(m // tm,),
        in_specs=[row(d), row(ATTN_WIDTH), row(pool_width), _const_spec(wo.shape), _const_spec(pw.shape),
                  _const_spec((1, pool_width)), _const_spec((1, d))],
        out_specs=row(d),
        out_shape=jax.ShapeDtypeStruct((m, d), F32),
        scratch_shapes=[pltpu.VMEM((POOL_HALO + tm, pool_width), F32)],
        compiler_params=_params("arbitrary"),
        name="mix_prompt",
    )(x, a, u, wo, pw, ps, g)


def _ffn_prompt_kernel(x_ref, gpre_ref, wg_ref, wv_ref, cwg_ref, cwv_ref, cbg_ref, cbv_ref, wd_ref, gpost_ref,
                       y_ref, lastg_ref, lastv_ref,
                       xn_scr, acc_scr, hg_scr, hv_scr, carryg_scr, carryv_scr, *, seq_len):
    i = pl.program_id(0)
    j = pl.program_id(1)
    tm = x_ref.shape[0]

    @pl.when(j == 0)
    def _():
        xn_scr[...] = _rms(x_ref[...], gpre_ref[...]).astype(BF16)
        acc_scr[...] = jnp.zeros(acc_scr.shape, F32)

    @pl.when(i == 0)
    def _():
        carryg_scr[j] = jnp.zeros(carryg_scr.shape[1:], F32)
        carryv_scr[j] = jnp.zeros(carryv_scr.shape[1:], F32)

    t = _pos_in_seq(i, tm, seq_len)
    in_seq1 = (t >= 1).astype(F32)
    in_seq2 = (t >= 2).astype(F32)
    seq_end = jnp.minimum(seq_len - (i * tm) % seq_len, tm)
    last_row0 = pl.multiple_of(seq_end, CONV_HALO)

    xn = xn_scr[...]
    acts = []
    for w_ref, cw_ref, cb_ref, h_scr, carry_scr, last_ref in (
            (wg_ref, cwg_ref, cbg_ref, hg_scr, carryg_scr, lastg_ref),
            (wv_ref, cwv_ref, cbv_ref, hv_scr, carryv_scr, lastv_ref)):
        h = _dot(xn, w_ref[...])
        h_scr[0:CONV_HALO, :] = carry_scr[j]
        h_scr[CONV_HALO:CONV_HALO + tm, :] = h
        carry_scr[j] = h[tm - CONV_HALO:, :]
        last_ref[0] = h_scr[pl.ds(last_row0, CONV_HALO), :]
        acts.append(cb_ref[...]
                    + cw_ref[0:1, :] * (in_seq2 * h_scr[CONV_HALO - 2:CONV_HALO - 2 + tm, :])
                    + cw_ref[1:2, :] * (in_seq1 * h_scr[CONV_HALO - 1:CONV_HALO - 1 + tm, :])
                    + cw_ref[2:3, :] * h)
    cg, cv = acts
    act = (cg * jax.nn.sigmoid(cg) * cv).astype(BF16)
    acc_scr[...] += _dot(act, wd_ref[...])

    @pl.when(j == pl.num_programs(1) - 1)
    def _():
        y_ref[...] = x_ref[...] + _rms(acc_scr[...], gpost_ref[...])


def _ffn_prompt(x, gpre, wup, cw, cb, wd, gpost, tm, tf, seq_len):
    m, d = x.shape
    d_ff = wd.shape[0]
    nj = d_ff // tf
    n_tiles = m // tm
    assert CONV_HALO <= tm <= seq_len and seq_len % CONV_HALO == 0 and tm % CONV_HALO == 0
    return pl.pallas_call(
        functools.partial(_ffn_prompt_kernel, seq_len=seq_len),
        grid=(n_tiles, nj),
        in_specs=[
            pl.BlockSpec((tm, d), lambda i, j: (i, 0)),
            pl.BlockSpec((1, d), lambda i, j: (0, 0)),
            pl.BlockSpec((d, tf), lambda i, j: (0, j)),
            pl.BlockSpec((d, tf), lambda i, j: (0, j + nj)),
            pl.BlockSpec((CONV_WIDTH, tf), lambda i, j: (0, j)),
            pl.BlockSpec((CONV_WIDTH, tf), lambda i, j: (0, j + nj)),
            pl.BlockSpec((1, tf), lambda i, j: (0, j)),
            pl.BlockSpec((1, tf), lambda i, j: (0, j + nj)),
            pl.BlockSpec((tf, d), lambda i, j: (j, 0)),
            pl.BlockSpec((1, d), lambda i, j: (0, 0)),
        ],
        out_specs=[
            pl.BlockSpec((tm, d), lambda i, j: (i, 0)),
            pl.BlockSpec((1, CONV_HALO, tf), lambda i, j: (i, 0, j)),
            pl.BlockSpec((1, CONV_HALO, tf), lambda i, j: (i, 0, j)),
        ],
        out_shape=[jax.ShapeDtypeStruct((m, d), F32),
                   jax.ShapeDtypeStruct((n_tiles, CONV_HALO, d_ff), F32),
                   jax.ShapeDtypeStruct((n_tiles, CONV_HALO, d_ff), F32)],
        scratch_shapes=[pltpu.VMEM((tm, d), BF16), pltpu.VMEM((tm, d), F32),
                        pltpu.VMEM((CONV_HALO + tm, tf), F32), pltpu.VMEM((CONV_HALO + tm, tf), F32),
                        pltpu.VMEM((nj, CONV_HALO, tf), F32), pltpu.VMEM((nj, CONV_HALO, tf), F32)],
        compiler_params=_params("arbitrary", "arbitrary"),
        name="ffn_prompt",
    )(x, gpre, wup, wup, cw, cw, cb, cb, wd, gpost)


def _idx_sample_kernel(pt_ref, iq_ref, iw_ref, new_ref, *refs):
    del pt_ref
    page_refs, o_ref, onew_ref = refs[:-2], refs[-2], refs[-1]
    iq = iq_ref[0]
    iw = iw_ref[0] * (IDX_DIM ** -0.5 * N_IDX_HEADS ** -0.5)

    def page_scores(keys):
        s = _dot_t(iq, keys.astype(BF16))
        return jnp.sum(jnp.maximum(s, 0.0) * iw, axis=0, keepdims=True)

    o_ref[0] = jnp.concatenate([page_scores(r[0, 0]) for r in page_refs], axis=0)
    onew_ref[0] = page_scores(new_ref[0])


def _page_specs(layer, block, n_pages):
    def spec(p):
        def index(b, c, pt):
            return (layer, pt[b * n_pages + c * PAGES_PER_STEP + p]) + (0,) * (len(block) - 2)
        return pl.BlockSpec(block, index)
    return [spec(p) for p in range(PAGES_PER_STEP)]


def _idx_sample(pt_flat, iq3, iw3, new_page, cache_kidx, layer, n_pages):
    db = iq3.shape[0]
    steps = n_pages // PAGES_PER_STEP
    grid_spec = pltpu.PrefetchScalarGridSpec(
        num_scalar_prefetch=1,
        grid=(db, steps),
        in_specs=[pl.BlockSpec((1, N_IDX_HEADS, IDX_DIM), lambda b, c, pt: (b, 0, 0)),
                  pl.BlockSpec((1, N_IDX_HEADS, 1), lambda b, c, pt: (b, 0, 0)),
                  pl.BlockSpec((1, PAGE_SIZE, IDX_DIM), lambda b, c, pt: (b, 0, 0))]
                 + _page_specs(layer, (1, 1, PAGE_SIZE, IDX_DIM), n_pages),
        out_specs=[pl.BlockSpec((1, PAGES_PER_STEP, PAGE_SIZE), lambda b, c, pt: (b, c, 0)),
                   pl.BlockSpec((1, 1, PAGE_SIZE), lambda b, c, pt: (b, 0, 0))],
    )
    return pl.pallas_call(
        _idx_sample_kernel,
        grid_spec=grid_spec,
        out_shape=[jax.ShapeDtypeStruct((db, n_pages, PAGE_SIZE), F32),
                   jax.ShapeDtypeStruct((db, 1, PAGE_SIZE), F32)],
        compiler_params=_params("arbitrary", "arbitrary"),
        name="idx_sample",
    )(pt_flat, iq3, iw3, new_page, *([cache_kidx] * PAGES_PER_STEP))


def _select_sample_kernel(sc_ref, scn_ref, bias_ref, biasn_ref, key_scr, *, topk):
    db = sc_ref.shape[0]
    key_new = _sort_key(scn_ref[:, 0:1])
    key_scr[...] = _sort_key(sc_ref[...])

    def count_ge(t):
        past = jnp.sum(jnp.where(key_scr[...] >= t, 1.0, 0.0), axis=1, keepdims=True)
        return past + jnp.where(key_new >= t, 1.0, 0.0)

    thr = _kth_largest_key(count_ge, db, topk)
    bias_ref[...] = jnp.where(key_scr[...] >= thr, 0.0, -jnp.inf)
    biasn_ref[...] = jnp.where(key_new >= thr, 0.0, -jnp.inf)


def _select_sample(scores, scores_new, topk):
    db, past = scores.shape
    return pl.pallas_call(
        functools.partial(_select_sample_kernel, topk=topk),
        out_shape=[jax.ShapeDtypeStruct((db, past), F32), jax.ShapeDtypeStruct((db, 1), F32)],
        scratch_shapes=[pltpu.VMEM((db, past), jnp.int32)],
        compiler_params=pltpu.CompilerParams(vmem_limit_bytes=V7X_VMEM_LIMIT_BYTES),
        name="select_sample",
    )(scores, scores_new)


def _attn_sample_kernel(pt_ref, q_ref, bias_ref, biasn_ref, kn_ref, vn_ref, *refs):
    del pt_ref
    k_refs = refs[:PAGES_PER_STEP]
    v_refs = refs[PAGES_PER_STEP:2 * PAGES_PER_STEP]
    o_ref, m_scr, l_scr, acc_scr = refs[2 * PAGES_PER_STEP:]
    c = pl.program_id(1)
    scale = HEAD_DIM ** -0.5

    @pl.when(c == 0)
    def _():
        m_scr[...] = jnp.full(m_scr.shape, -jnp.inf, F32)
        l_scr[...] = jnp.zeros(l_scr.shape, F32)
        acc_scr[...] = jnp.zeros(acc_scr.shape, F32)

    def update(g, s, v):
        m_old = m_scr[g]
        m_new = jnp.maximum(m_old, jnp.max(s, axis=-1, keepdims=True))
        safe = jnp.where(m_new == -jnp.inf, 0.0, m_new)
        alpha = jnp.exp(m_old - safe)
        p = jnp.exp(s - safe)
        l_scr[g] = alpha * l_scr[g] + jnp.sum(p, axis=-1, keepdims=True)
        acc_scr[g] = alpha * acc_scr[g] + _dot(p.astype(BF16), v)
        m_scr[g] = m_new

    bias = jnp.concatenate([bias_ref[0, p:p + 1, :] for p in range(PAGES_PER_STEP)], axis=1)
    for g in range(N_KV_HEADS):
        kg = jnp.concatenate([r[0, 0, pl.ds(g, PAGE_SIZE, stride=N_KV_HEADS), :] for r in k_refs], axis=0)
        vg = jnp.concatenate([r[0, 0, pl.ds(g, PAGE_SIZE, stride=N_KV_HEADS), :] for r in v_refs], axis=0)
        s = _dot_t(q_ref[0, g], kg.astype(BF16)) * scale + bias
        update(g, s, vg.astype(BF16))

    @pl.when(c == pl.num_programs(1) - 1)
    def _():
        for g in range(N_KV_HEADS):
            kn = kn_ref[0, :, g * HEAD_DIM:(g + 1) * HEAD_DIM].astype(BF16)
            vn = vn_ref[0, :, g * HEAD_DIM:(g + 1) * HEAD_DIM].astype(BF16)
            qf = q_ref[0, g].astype(F32)
            s = jnp.sum(qf * kn.astype(F32), axis=-1, keepdims=True) * scale + biasn_ref[0]
            m_old = m_scr[g]
            m_new = jnp.maximum(m_old, s)
            alpha = jnp.exp(m_old - m_new)
            p = jnp.exp(s - m_new)
            l = alpha * l_scr[g] + p
            acc = alpha * acc_scr[g] + p.astype(BF16).astype(F32) * vn.astype(F32)
            o_ref[0, g] = acc / l


def _attn_sample(pt_flat, q4, bias3, bias_new, k_new, v_new, cache_k, cache_v, layer, n_pages):
    db, _, rows, _ = q4.shape
    steps = n_pages // PAGES_PER_STEP
    page_block = (1, 1, PAGE_SIZE * N_KV_HEADS, HEAD_DIM)
    grid_spec = pltpu.PrefetchScalarGridSpec(
        num_scalar_prefetch=1,
        grid=(db, steps),
        in_specs=[pl.BlockSpec((1, N_KV_HEADS, rows, HEAD_DIM), lambda b, c, pt: (b, 0, 0, 0)),
                  pl.BlockSpec((1, PAGES_PER_STEP, PAGE_SIZE), lambda b, c, pt: (b, c, 0)),
                  pl.BlockSpec((1, 1, 1), lambda b, c, pt: (b, 0, 0)),
                  pl.BlockSpec((1, 1, KV_WIDTH), lambda b, c, pt: (b, 0, 0)),
                  pl.BlockSpec((1, 1, KV_WIDTH), lambda b, c, pt: (b, 0, 0))]
                 + _page_specs(layer, page_block, n_pages) + _page_specs(layer, page_block, n_pages),
        out_specs=pl.BlockSpec((1, N_KV_HEADS, rows, HEAD_DIM), lambda b, c, pt: (b, 0, 0, 0)),
        scratch_shapes=[pltpu.VMEM((N_KV_HEADS, rows, 1), F32),
                        pltpu.VMEM((N_KV_HEADS, rows, 1), F32),
                        pltpu.VMEM((N_KV_HEADS, rows, HEAD_DIM), F32)],
    )
    return pl.pallas_call(
        _attn_sample_kernel,
        grid_spec=grid_spec,
        out_shape=jax.ShapeDtypeStruct((db, N_KV_HEADS, rows, HEAD_DIM), F32),
        compiler_params=_params("arbitrary", "arbitrary"),
        name="attn_sample",
    )(pt_flat, q4, bias3, bias_new, k_new, v_new, *([cache_k] * PAGES_PER_STEP), *([cache_v] * PAGES_PER_STEP))


def _mix_sample_kernel(x_ref, a_ref, e_ref, wo_ref, pw_ref, ps_ref, g_ref, y_ref):
    group = pw_ref.shape[1]
    n_rows = e_ref.shape[0]
    d_parts = []
    for g, w in enumerate(POOL_WINDOWS):
        cols = slice(g * group, (g + 1) * group)
        ug = e_ref[n_rows - 1, :, cols]
        wsum = ug
        for s in range(1, w):
            wsum = wsum + e_ref[n_rows - 1 - s, :, cols]
        d_parts.append(wsum / float(w) - ug)
    p = _pool_project(d_parts, pw_ref, ps_ref)
    o = _dot(a_ref[...].astype(BF16), wo_ref[0:ATTN_WIDTH, :]) + _dot(p, wo_ref[ATTN_WIDTH:, :])
    y_ref[...] = x_ref[...] + _rms(o, g_ref[...])


def _mix_sample(x, a, e, wo, pw, ps, g):
    return pl.pallas_call(
        _mix_sample_kernel,
        out_shape=jax.ShapeDtypeStruct(x.shape, F32),
        compiler_params=pltpu.CompilerParams(vmem_limit_bytes=V7X_VMEM_LIMIT_BYTES),
        name="mix_sample",
    )(x, a, e, wo, pw, ps, g)


def _ffn_sample_kernel(x_ref, gpre_ref, wg_ref, wv_ref, cwg_ref, cwv_ref, cbg_ref, cbv_ref, sg_ref, sv_ref,
                       wd_ref, gpost_ref, y_ref, hg_ref, hv_ref, xn_scr, acc_scr):
    j = pl.program_id(0)

    @pl.when(j == 0)
    def _():
        xn_scr[...] = _rms(x_ref[...], gpre_ref[...]).astype(BF16)
        acc_scr[...] = jnp.zeros(acc_scr.shape, F32)

    xn = xn_scr[...]
    hg = _dot(xn, wg_ref[...])
    hv = _dot(xn, wv_ref[...])
    hg_ref[...] = hg
    hv_ref[...] = hv

    def conv(h, s_ref, cw_ref, cb_ref):
        return cb_ref[...] + cw_ref[0:1, :] * s_ref[0] + cw_ref[1:2, :] * s_ref[1] + cw_ref[2:3, :] * h

    cg = conv(hg, sg_ref, cwg_ref, cbg_ref)
    cv = conv(hv, sv_ref, cwv_ref, cbv_ref)
    act = (cg * jax.nn.sigmoid(cg) * cv).astype(BF16)
    acc_scr[...] += _dot(act, wd_ref[...])

    @pl.when(j == pl.num_programs(0) - 1)
    def _():
        y_ref[...] = x_ref[...] + _rms(acc_scr[...], gpost_ref[...])


def _ffn_sample(x, gpre, wup, cw, cb, state, wd, gpost, tf):
    db, d = x.shape
    d_ff = wd.shape[0]
    nj = d_ff // tf
    n_state = state.shape[0]
    return pl.pallas_call(
        _ffn_sample_kernel,
        grid=(nj,),
        in_specs=[
            pl.BlockSpec((db, d), lambda j: (0, 0)),
            pl.BlockSpec((1, d), lambda j: (0, 0)),
            pl.BlockSpec((d, tf), lambda j: (0, j)),
            pl.BlockSpec((d, tf), lambda j: (0, j + nj)),
            pl.BlockSpec((CONV_WIDTH, tf), lambda j: (0, j)),
            pl.BlockSpec((CONV_WIDTH, tf), lambda j: (0, j + nj)),
            pl.BlockSpec((1, tf), lambda j: (0, j)),
            pl.BlockSpec((1, tf), lambda j: (0, j + nj)),
            pl.BlockSpec((n_state, db, tf), lambda j: (0, 0, j)),
            pl.BlockSpec((n_state, db, tf), lambda j: (0, 0, j + nj)),
            pl.BlockSpec((tf, d), lambda j: (j, 0)),
            pl.BlockSpec((1, d), lambda j: (0, 0)),
        ],
        out_specs=[pl.BlockSpec((db, d), lambda j: (0, 0)),
                   pl.BlockSpec((db, tf), lambda j: (0, j)),
                   pl.BlockSpec((db, tf), lambda j: (0, j))],
        out_shape=[jax.ShapeDtypeStruct((db, d), F32), jax.ShapeDtypeStruct((db, d_ff), F32),
                   jax.ShapeDtypeStruct((db, d_ff), F32)],
        scratch_shapes=[pltpu.VMEM((db, d), BF16), pltpu.VMEM((db, d), F32)],
        compiler_params=_params("arbitrary"),
        name="ffn_sample",
    )(x, gpre, wup, wup, cw, cw, cb, cb, state, state, wd, gpost)


def _split_w_in(w_in_l):
    d = w_in_l.shape[0]
    q_end = ATTN_WIDTH
    k_end = q_end + KV_WIDTH
    v_end = k_end + KV_WIDTH
    iq_end = v_end + N_IDX_HEADS * IDX_DIM
    ik_end = iq_end + IDX_DIM
    iw_end = ik_end + N_IDX_HEADS
    w = w_in_l.astype(BF16)
    w_ik = w[:, iq_end:ik_end]
    zeros = lambda c: jnp.zeros((d, c), BF16)
    idx = jnp.concatenate([w_ik, zeros(V7X_LANES - IDX_DIM), zeros(V7X_LANES - IDX_DIM), w_ik,
                           w[:, ik_end:iw_end], zeros(V7X_LANES - N_IDX_HEADS)], axis=1)
    return {"q": w[:, :q_end], "kv": w[:, q_end:v_end], "iq": w[:, v_end:iq_end], "idx": idx, "u": w[:, iw_end:]}


def kernel(x_prompt, x_sample, cache_k, cache_v, cache_kidx, state_pool, state_conv, page_table, meta_tokens,
           w_in, w_o, pool_w, pool_scale, g_mix_pre, g_mix_post, g_ffn_pre, g_ffn_post, w_up, conv_w, conv_b,
           w_down):
    b, seq, d = x_prompt.shape
    db, ds, _ = x_sample.shape
    assert ds == 1, "the sample kernels handle one new token per sequence"
    depth = w_in.shape[0]
    n_pool = cache_k.shape[1]
    n_pages = page_table.shape[1]
    assert cache_k.shape[2] == PAGE_SIZE and n_pages % PAGES_PER_STEP == 0
    past = n_pages * PAGE_SIZE
    t_len = seq + N_META
    d_ff = w_down.shape[1]
    pool_width = d - ATTN_WIDTH
    topk_p = min(TOPK_MAX, t_len // 4)
    topk_s = min(TOPK_MAX, (past + ds) // 4)
    assert past >= max(POOL_WINDOWS)

    m_rows = b * t_len
    tm_proj = _row_tile(m_rows, min(TM_PROJ_CAP, t_len))
    tm_ffn = _row_tile(m_rows, min(TM_FFN_CAP, t_len))
    tf = _ff_tile(d_ff)
    sub8 = V7X_SUBLANES_F32
    heads_pad = -(-HEADS_PER_KV // sub8) * sub8

    meta = jnp.broadcast_to(meta_tokens[None].astype(x_prompt.dtype), (b, N_META, d))
    xp = jnp.concatenate([meta, x_prompt], axis=1).reshape(b * t_len, d)
    xs = x_sample.reshape(db, d)
    pt_flat = page_table.reshape(-1).astype(jnp.int32)
    ck = cache_k.reshape(depth, n_pool, PAGE_SIZE * N_KV_HEADS, HEAD_DIM)
    cv = cache_v.reshape(depth, n_pool, PAGE_SIZE * N_KV_HEADS, HEAD_DIM)

    row2 = lambda a: a.reshape(1, -1)
    outs = {name: [] for name in ("kp", "vp", "ikp", "poolp", "convp", "ks", "vs", "iks", "pools", "convs")}
    for l in range(depth):
        w = _split_w_in(w_in[l])
        wo = w_o[l].astype(BF16)
        pw = pool_w[l].astype(BF16)
        ps = row2(pool_scale[l])
        wup = w_up[l].astype(BF16)
        wd = w_down[l].astype(BF16)
        cw = conv_w[l]
        cb = row2(conv_b[l])

        q, k, v, kb, vb, iq, ik, ik2, iw, u = _in_proj(xp, row2(g_mix_pre[l]), w, tm_proj)
        b3 = lambda a: a.reshape(b, t_len, a.shape[-1])
        a = _attn_prompt(b3(q), b3(iq), b3(iw), b3(kb), b3(vb), b3(ik2), topk_p)
        xp = _mix_prompt(xp, a.reshape(m_rows, ATTN_WIDTH), u, wo, pw, ps, row2(g_mix_post[l]), tm_proj, t_len)
        xp, last_g, last_v = _ffn_prompt(xp, row2(g_ffn_pre[l]), wup, cw, cb, wd, row2(g_ffn_post[l]),
                                         tm_ffn, tf, t_len)
        outs["kp"].append(k.reshape(b, t_len, N_KV_HEADS, HEAD_DIM))
        outs["vp"].append(v.reshape(b, t_len, N_KV_HEADS, HEAD_DIM))
        outs["ikp"].append(ik.reshape(b, t_len, IDX_DIM))
        outs["poolp"].append(u.reshape(b, t_len, pool_width)[:, t_len - POOL_BUF:])
        end_tiles = [((s + 1) * t_len - 1) // tm_ffn for s in range(b)]
        seq_last = lambda h: jnp.stack([h[i, CONV_HALO - (CONV_WIDTH - 1):] for i in end_tiles])
        outs["convp"].append(jnp.concatenate([seq_last(last_g), seq_last(last_v)], axis=-1))

        q, k, v, _, _, iq, ik, _, iw, u = _in_proj(xs, row2(g_mix_pre[l]), w, db)
        iq3 = iq.reshape(db, N_IDX_HEADS, IDX_DIM)
        new_page = jnp.pad(ik[:, None, :], ((0, 0), (0, PAGE_SIZE - 1), (0, 0)))
        scores, scores_new = _idx_sample(pt_flat, iq3, iw.reshape(db, N_IDX_HEADS, 1), new_page, cache_kidx, l,
                                         n_pages)
        bias, bias_new = _select_sample(scores.reshape(db, past), scores_new.reshape(db, PAGE_SIZE), topk_s)
        q4 = jnp.pad(q.reshape(db, N_KV_HEADS, HEADS_PER_KV, HEAD_DIM),
                     ((0, 0), (0, 0), (0, heads_pad - HEADS_PER_KV), (0, 0)))
        a = _attn_sample(pt_flat, q4, bias.reshape(db, n_pages, PAGE_SIZE), bias_new.reshape(db, 1, 1),
                         k.reshape(db, 1, KV_WIDTH), v.reshape(db, 1, KV_WIDTH), ck, cv, l, n_pages)
        a = a[:, :, :HEADS_PER_KV].reshape(db, ATTN_WIDTH)
        e = jnp.concatenate([jnp.swapaxes(state_pool[l], 0, 1).astype(u.dtype), u[None]], axis=0)
        xs = _mix_sample(xs, a, e, wo, pw, ps, row2(g_mix_post[l]))
        xs, hg, hv = _ffn_sample(xs, row2(g_ffn_pre[l]), wup, cw, cb, jnp.swapaxes(state_conv[l], 0, 1), wd,
                                 row2(g_ffn_post[l]), tf)
        outs["ks"].append(k.reshape(db, ds, N_KV_HEADS, HEAD_DIM))
        outs["vs"].append(v.reshape(db, ds, N_KV_HEADS, HEAD_DIM))
        outs["iks"].append(ik.reshape(db, ds, IDX_DIM))
        outs["pools"].append(jnp.concatenate([state_pool[l].astype(u.dtype), u[:, None, :]], axis=1)[:, 1:])
        h = jnp.concatenate([hg, hv], axis=-1)[:, None, :]
        outs["convs"].append(jnp.concatenate([state_conv[l].astype(h.dtype), h], axis=1)[:, -(CONV_WIDTH - 1):])

    y_prompt = xp.reshape(b, t_len, d)[:, N_META:]
    y_sample = xs.reshape(db, ds, d)
    stack = lambda name: jnp.stack(outs[name])
    return (y_prompt, y_sample, stack("kp"), stack("vp"), stack("ikp"), stack("poolp"), stack("convp"),
            stack("ks"), stack("vs"), stack("iks"), stack("pools"), stack("convs"))
```

```python
import functools

import jax
import jax.numpy as jnp
from jax import lax
from jax.experimental import pallas as pl
from jax.experimental.pallas import tpu as pltpu

N_META = 16
N_HEADS = 8
N_KV_HEADS = 2
HEAD_DIM = 128
N_IDX_HEADS = 16
IDX_DIM = 64
TOPK_MAX = 256
POOL_WINDOWS = (2, 4, 8, 16)
CONV_WIDTH = 3
PAGE_SIZE = 128
Q_BLOCK = 128
RMS_EPS = 1e-6

ATTN_WIDTH = N_HEADS * HEAD_DIM
KV_WIDTH = N_KV_HEADS * HEAD_DIM
HEADS_PER_KV = N_HEADS // N_KV_HEADS
POOL_BUF = max(POOL_WINDOWS) - 1

V7X_LANES = 128
V7X_SUBLANES_F32 = 8
V7X_SUBLANES_BF16 = 16
V7X_VMEM_LIMIT_BYTES = 56 * 1024 * 1024

INT_MIN = -(2 ** 31)
F32 = jnp.float32
BF16 = jnp.bfloat16

CONV_HALO = V7X_SUBLANES_F32
POOL_HALO = 2 * V7X_SUBLANES_F32
PAGES_PER_STEP = 8
KEY_CHUNK = 256
TM_PROJ_CAP = 384
TM_FFN_CAP = 688


def _params(*sem):
    return pltpu.CompilerParams(dimension_semantics=sem, vmem_limit_bytes=V7X_VMEM_LIMIT_BYTES)


def _row_tile(seq_len, cap):
    best = None
    for d in range(V7X_SUBLANES_BF16, min(seq_len, cap) + 1, V7X_SUBLANES_BF16):
        if seq_len % d == 0:
            best = d
    assert best is not None, seq_len
    return best


def _ff_tile(d_ff, cap=512):
    best = None
    for d in range(V7X_LANES, min(d_ff, cap) + 1, V7X_LANES):
        if d_ff % d == 0:
            best = d
    assert best is not None, d_ff
    return best


def _rms(x, g):
    return x * lax.rsqrt(jnp.mean(x * x, axis=-1, keepdims=True) + RMS_EPS) * g


def _dot(a, b):
    return jnp.dot(a, b, preferred_element_type=F32)


def _dot_t(a, b):
    return lax.dot_general(a, b, (((1,), (1,)), ((), ())), preferred_element_type=F32)


def _const_spec(shape):
    n = len(shape)
    return pl.BlockSpec(shape, lambda *_: (0,) * n)


def _in_proj_kernel(x_ref, g_ref, wq_ref, wkv_ref, wiq_ref, widx_ref, wu_ref,
                    q_ref, k_ref, v_ref, kb_ref, vb_ref, iq_ref, ik_ref, ik2_ref, iw_ref, u_ref):
    hn = _rms(x_ref[...], g_ref[...]).astype(BF16)
    q_ref[...] = _dot(hn, wq_ref[...]).astype(BF16)
    kv = _dot(hn, wkv_ref[...])
    k_ref[...] = kv[:, :KV_WIDTH]
    v_ref[...] = kv[:, KV_WIDTH:]
    kb_ref[...] = kv[:, :KV_WIDTH].astype(BF16)
    vb_ref[...] = kv[:, KV_WIDTH:].astype(BF16)
    iq_ref[...] = _dot(hn, wiq_ref[...]).astype(BF16)
    idx = _dot(hn, widx_ref[...])
    ik_ref[...] = idx[:, :IDX_DIM]
    ik2_ref[...] = idx[:, :2 * V7X_LANES].astype(BF16)
    iw_ref[...] = idx[:, 2 * V7X_LANES:2 * V7X_LANES + N_IDX_HEADS]
    u_ref[...] = _dot(hn, wu_ref[...])


def _in_proj(x, g, w, tm):
    m, d = x.shape
    pool_width = w["u"].shape[1]
    row = lambda c: pl.BlockSpec((tm, c), lambda i: (i, 0))
    outs = [
        (ATTN_WIDTH, BF16), (KV_WIDTH, F32), (KV_WIDTH, F32), (KV_WIDTH, BF16), (KV_WIDTH, BF16),
        (N_IDX_HEADS * IDX_DIM, BF16), (IDX_DIM, F32), (2 * V7X_LANES, BF16), (N_IDX_HEADS, F32),
        (pool_width, F32),
    ]
    return pl.pallas_call(
        _in_proj_kernel,
        grid=(m // tm,),
        in_specs=[row(d), _const_spec((1, d)), _const_spec(w["q"].shape), _const_spec(w["kv"].shape),
                  _const_spec(w["iq"].shape), _const_spec(w["idx"].shape), _const_spec(w["u"].shape)],
        out_specs=[row(c) for c, _ in outs],
        out_shape=[jax.ShapeDtypeStruct((m, c), dt) for c, dt in outs],
        compiler_params=_params("arbitrary"),
        name="in_proj",
    )(x, g, w["q"], w["kv"], w["iq"], w["idx"], w["u"])


def _sort_key(x):
    bits = lax.bitcast_convert_type(x, jnp.int32)
    return bits ^ (lax.shift_right_arithmetic(bits, 31) & jnp.int32(0x7FFFFFFF))


def _kth_largest_key(count_ge, shape, k):
    def body(it, t):
        cand = t | lax.shift_left(jnp.int32(1), 31 - it)
        cnt = count_ge(cand ^ jnp.int32(INT_MIN))
        return jnp.where(cnt >= k, cand, t)

    t = lax.fori_loop(0, 32, body, jnp.zeros(shape, jnp.int32))
    return t ^ jnp.int32(INT_MIN)


def _attn_prompt_kernel(q_ref, iq_ref, iwt_ref, kb_ref, vb_ref, ik2_ref, o_ref,
                        k_scr, v_scr, ik_scr, key_scr, thr_scr, bias_scr, s_scr, m_scr, l_scr, acc_scr, *, topk):
    i = pl.program_id(1)
    t_len = kb_ref.shape[1]
    n_chunks, ch, _ = key_scr.shape
    tk = n_chunks * ch
    halves = ch // V7X_LANES
    rows = HEADS_PER_KV * Q_BLOCK
    nc = ((i + 1) * Q_BLOCK + ch - 1) // ch

    @pl.when(i == 0)
    def _():
        k_scr[0:t_len, :] = kb_ref[0]
        v_scr[0:t_len, :] = vb_ref[0]
        ik_scr[0, 0:t_len, :] = ik2_ref[0, :, 0:V7X_LANES]
        ik_scr[1, 0:t_len, :] = ik2_ref[0, :, V7X_LANES:]
        if tk > t_len:
            k_scr[t_len:tk, :] = jnp.zeros((tk - t_len, k_scr.shape[1]), BF16)
            v_scr[t_len:tk, :] = jnp.zeros((tk - t_len, v_scr.shape[1]), BF16)
            ik_scr[0, t_len:tk, :] = jnp.zeros((tk - t_len, V7X_LANES), BF16)
            ik_scr[1, t_len:tk, :] = jnp.zeros((tk - t_len, V7X_LANES), BF16)

    iw = iwt_ref[0] * (IDX_DIM ** -0.5 * N_IDX_HEADS ** -0.5)
    iq2 = [jnp.concatenate([iq_ref[0, :, (2 * jj + e) * V7X_LANES:(2 * jj + e + 1) * V7X_LANES] for e in range(2)],
                           axis=0) for jj in range(N_IDX_HEADS // 4)]
    qpos = i * Q_BLOCK + lax.broadcasted_iota(jnp.int32, (1, Q_BLOCK), 1)

    def index_chunk(c, carry):
        r0 = pl.multiple_of(c * ch, ch)
        keys = jnp.concatenate([ik_scr[0, pl.ds(r0, ch), :], ik_scr[1, pl.ds(r0, ch), :]], axis=0)
        acc = jnp.zeros((ch, Q_BLOCK), F32)
        for jj in range(N_IDX_HEADS // 4):
            s = _dot_t(keys, iq2[jj])
            for e in range(2):
                for half in range(2):
                    h = 2 * (2 * jj + e) + half
                    blk = s[half * ch:(half + 1) * ch, e * Q_BLOCK:(e + 1) * Q_BLOCK]
                    acc = acc + jnp.maximum(blk, 0.0) * iw[h:h + 1, :]
        kpos = r0 + lax.broadcasted_iota(jnp.int32, (ch, 1), 0)
        key_scr[c] = jnp.where(kpos <= qpos, _sort_key(acc), jnp.int32(INT_MIN))
        return carry

    lax.fori_loop(0, nc, index_chunk, 0)

    for n_static in range(1, n_chunks + 1):
        @pl.when(nc == n_static)
        def _(n_static=n_static):
            def count_ge(t):
                part = jnp.zeros((V7X_SUBLANES_F32, Q_BLOCK), F32)
                for c in range(n_static):
                    hit = jnp.where(key_scr[c] >= t, 1.0, 0.0)
                    part = part + jnp.sum(hit.reshape(ch // V7X_SUBLANES_F32, V7X_SUBLANES_F32, Q_BLOCK), axis=0)
                return jnp.sum(part, axis=0, keepdims=True)

            t = _kth_largest_key(count_ge, (1, Q_BLOCK), topk)
            thr_scr[...] = jnp.broadcast_to(jnp.maximum(t, jnp.int32(INT_MIN + 1)), thr_scr.shape)

    thr = thr_scr[0:1, :]

    def bias_chunk(c, carry):
        bias_t = jnp.where(key_scr[c] >= thr, 0.0, -jnp.inf)
        for r in range(halves):
            bias_scr[c, :, r * V7X_LANES:(r + 1) * V7X_LANES] = bias_t[r * V7X_LANES:(r + 1) * V7X_LANES, :].T
        return carry

    lax.fori_loop(0, nc, bias_chunk, 0)

    scale = HEAD_DIM ** -0.5
    gcols = [slice(g * HEAD_DIM, (g + 1) * HEAD_DIM) for g in range(N_KV_HEADS)]
    qgs = [jnp.concatenate([q_ref[0, :, (g * HEADS_PER_KV + r) * HEAD_DIM:(g * HEADS_PER_KV + r + 1) * HEAD_DIM]
                            for r in range(HEADS_PER_KV)], axis=0) for g in range(N_KV_HEADS)]
    lane_tiles = lambda x: [x[:, r * V7X_LANES:(r + 1) * V7X_LANES] for r in range(halves)]

    m_scr[...] = jnp.full(m_scr.shape, -jnp.inf, F32)

    def score_chunk(c, carry):
        r0 = pl.multiple_of(c * ch, ch)
        bias = bias_scr[c][None]
        for g in range(N_KV_HEADS):
            s = _dot_t(qgs[g], k_scr[pl.ds(r0, ch), gcols[g]])
            s = (s.reshape(HEADS_PER_KV, Q_BLOCK, ch) * scale + bias).reshape(rows, ch)
            s_scr[g, c] = s
            m = m_scr[g]
            for tile in lane_tiles(s):
                m = jnp.maximum(m, tile)
            m_scr[g] = m
        return carry

    lax.fori_loop(0, nc, score_chunk, 0)
    for g in range(N_KV_HEADS):
        m_scr[g] = jnp.broadcast_to(jnp.max(m_scr[g], axis=1, keepdims=True), m_scr.shape[1:])
    l_scr[...] = jnp.zeros(l_scr.shape, F32)
    acc_scr[...] = jnp.zeros(acc_scr.shape, F32)

    def value_chunk(c, carry):
        r0 = pl.multiple_of(c * ch, ch)
        for g in range(N_KV_HEADS):
            m = m_scr[g]
            p_tiles = [jnp.exp(tile - m) for tile in lane_tiles(s_scr[g, c])]
            l = l_scr[g]
            for tile in p_tiles:
                l = l + tile
            l_scr[g] = l
            p = jnp.concatenate(p_tiles, axis=1).astype(BF16)
            acc_scr[g] += _dot(p, v_scr[pl.ds(r0, ch), gcols[g]])
        return carry

    lax.fori_loop(0, nc, value_chunk, 0)
    for g in range(N_KV_HEADS):
        o = acc_scr[g] / jnp.sum(l_scr[g], axis=1, keepdims=True)
        for r in range(HEADS_PER_KV):
            h = g * HEADS_PER_KV + r
            o_ref[0, :, h * HEAD_DIM:(h + 1) * HEAD_DIM] = o[r * Q_BLOCK:(r + 1) * Q_BLOCK].astype(o_ref.dtype)


def _attn_prompt(q, iq, iwt, kb, vb, ik2, topk):
    b, t_len, _ = q.shape
    n_blk = pl.cdiv(t_len, Q_BLOCK)
    n_chunks = pl.cdiv(n_blk * Q_BLOCK, KEY_CHUNK)
    tk = n_chunks * KEY_CHUNK
    rows = HEADS_PER_KV * Q_BLOCK
    qblk = lambda c: pl.BlockSpec((1, Q_BLOCK, c), lambda bi, i: (bi, i, 0))
    full = lambda c: pl.BlockSpec((1, t_len, c), lambda bi, i: (bi, 0, 0))
    return pl.pallas_call(
        functools.partial(_attn_prompt_kernel, topk=topk),
        grid=(b, n_blk),
        in_specs=[qblk(ATTN_WIDTH), qblk(N_IDX_HEADS * IDX_DIM),
                  pl.BlockSpec((1, N_IDX_HEADS, Q_BLOCK), lambda bi, i: (bi, 0, i)),
                  full(KV_WIDTH), full(KV_WIDTH), full(2 * V7X_LANES)],
        out_specs=qblk(ATTN_WIDTH),
        out_shape=jax.ShapeDtypeStruct((b, t_len, ATTN_WIDTH), BF16),
        scratch_shapes=[pltpu.VMEM((tk, KV_WIDTH), BF16), pltpu.VMEM((tk, KV_WIDTH), BF16),
                        pltpu.VMEM((2, tk, V7X_LANES), BF16),
                        pltpu.VMEM((n_chunks, KEY_CHUNK, Q_BLOCK), jnp.int32),
                        pltpu.VMEM((V7X_SUBLANES_F32, Q_BLOCK), jnp.int32),
                        pltpu.VMEM((n_chunks, Q_BLOCK, KEY_CHUNK), F32),
                        pltpu.VMEM((N_KV_HEADS, n_chunks, rows, KEY_CHUNK), F32),
                        pltpu.VMEM((N_KV_HEADS, rows, V7X_LANES), F32),
                        pltpu.VMEM((N_KV_HEADS, rows, V7X_LANES), F32),
                        pltpu.VMEM((N_KV_HEADS, rows, HEAD_DIM), F32)],
        compiler_params=_params("arbitrary", "arbitrary"),
        name="attn_prompt",
    )(q, iq, iwt, kb, vb, ik2)


def _pool_project(d_parts, pw_ref, ps_ref):
    group = d_parts[0].shape[1]
    outs = []
    for g, d in enumerate(d_parts):
        outs.append(_dot(d.astype(BF16), pw_ref[g]) * ps_ref[:, g * group:(g + 1) * group])
    return jnp.concatenate(outs, axis=1).astype(BF16)


def _pos_in_seq(i, tm, seq_len):
    t = (i * tm) % seq_len + lax.broadcasted_iota(jnp.int32, (tm, 1), 0)
    return jnp.where(t >= seq_len, t - seq_len, t)


def _mix_prompt_kernel(x_ref, a_ref, u_ref, wo_ref, pw_ref, ps_ref, g_ref, y_ref, e_scr, *, seq_len):
    i = pl.program_id(0)
    tm = x_ref.shape[0]
    group = pw_ref.shape[1]

    @pl.when(i == 0)
    def _():
        e_scr[0:POOL_HALO, :] = jnp.zeros((POOL_HALO, e_scr.shape[1]), F32)

    u = u_ref[...]
    e_scr[POOL_HALO:POOL_HALO + tm, :] = u
    t = _pos_in_seq(i, tm, seq_len)
    in_seq = [None] + [(t >= s).astype(F32) for s in range(1, max(POOL_WINDOWS))]
    d_parts = []
    for g, w in enumerate(POOL_WINDOWS):
        cols = slice(g * group, (g + 1) * group)
        ug = u[:, cols]
        wsum = ug
        for s in range(1, w):
            wsum = wsum + in_seq[s] * e_scr[POOL_HALO - s:POOL_HALO - s + tm, cols]
        cnt = jnp.minimum(t + 1, w).astype(F32)
        d_parts.append(wsum / cnt - ug)
    e_scr[0:POOL_HALO, :] = u[tm - POOL_HALO:, :]
    p = _pool_project(d_parts, pw_ref, ps_ref)
    o = _dot(a_ref[...], wo_ref[0:ATTN_WIDTH, :]) + _dot(p, wo_ref[ATTN_WIDTH:, :])
    y_ref[...] = x_ref[...] + _rms(o, g_ref[...])


def _mix_prompt(x, a, u, wo, pw, ps, g, tm, seq_len):
    m, d = x.shape
    pool_width = u.shape[1]
    assert POOL_HALO <= tm <= seq_len
    row = lambda c: pl.BlockSpec((tm, c), lambda i: (i, 0))
    return pl.pallas_call(
        functools.partial(_mix_prompt_kernel, seq_len=seq_len),
        grid=(m // tm,),
        in_specs=[row(d), row(ATTN_WIDTH), row(pool_width), _const_spec(wo.shape), _const_spec(pw.shape),
                  _const_spec((1, pool_width)), _const_spec((1, d))],
        out_specs=row(d),
        out_shape=jax.ShapeDtypeStruct((m, d), F32),
        scratch_shapes=[pltpu.VMEM((POOL_HALO + tm, pool_width), F32)],
        compiler_params=_params("arbitrary"),
        name="mix_prompt",
    )(x, a, u, wo, pw, ps, g)


def _ffn_prompt_kernel(x_ref, gpre_ref, wg_ref, wv_ref, cwg_ref, cwv_ref, cbg_ref, cbv_ref, wd_ref, gpost_ref,
                       y_ref, lastg_ref, lastv_ref,
                       xn_scr, acc_scr, hg_scr, hv_scr, carryg_scr, carryv_scr, *, seq_len):
    i = pl.program_id(0)
    j = pl.program_id(1)
    tm = x_ref.shape[0]

    @pl.when(j == 0)
    def _():
        xn_scr[...] = _rms(x_ref[...], gpre_ref[...]).astype(BF16)
        acc_scr[...] = jnp.zeros(acc_scr.shape, F32)

    @pl.when(i == 0)
    def _():
        carryg_scr[j] = jnp.zeros(carryg_scr.shape[1:], F32)
        carryv_scr[j] = jnp.zeros(carryv_scr.shape[1:], F32)

    t = _pos_in_seq(i, tm, seq_len)
    in_seq1 = (t >= 1).astype(F32)
    in_seq2 = (t >= 2).astype(F32)
    seq_end = jnp.minimum(seq_len - (i * tm) % seq_len, tm)
    last_row0 = pl.multiple_of(seq_end, CONV_HALO)

    xn = xn_scr[...]
    acts = []
    for w_ref, cw_ref, cb_ref, h_scr, carry_scr, last_ref in (
            (wg_ref, cwg_ref, cbg_ref, hg_scr, carryg_scr, lastg_ref),
            (wv_ref, cwv_ref, cbv_ref, hv_scr, carryv_scr, lastv_ref)):
        h = _dot(xn, w_ref[...])
        h_scr[0:CONV_HALO, :] = carry_scr[j]
        h_scr[CONV_HALO:CONV_HALO + tm, :] = h
        carry_scr[j] = h[tm - CONV_HALO:, :]
        last_ref[0] = h_scr[pl.ds(last_row0, CONV_HALO), :]
        acts.append(cb_ref[...]
                    + cw_ref[0:1, :] * (in_seq2 * h_scr[CONV_HALO - 2:CONV_HALO - 2 + tm, :])
                    + cw_ref[1:2, :] * (in_seq1 * h_scr[CONV_HALO - 1:CONV_HALO - 1 + tm, :])
                    + cw_ref[2:3, :] * h)
    cg, cv = acts
    act = (cg * jax.nn.sigmoid(cg) * cv).astype(BF16)
    acc_scr[...] += _dot(act, wd_ref[...])

    @pl.when(j == pl.num_programs(1) - 1)
    def _():
        y_ref[...] = x_ref[...] + _rms(acc_scr[...], gpost_ref[...])


def _ffn_prompt(x, gpre, wup, cw, cb, wd, gpost, layer, tm, tf, seq_len):
    m, d = x.shape
    d_ff = wd.shape[1]
    nj = d_ff // tf
    n_tiles = m // tm
    assert CONV_HALO <= tm <= seq_len and seq_len % CONV_HALO == 0 and tm % CONV_HALO == 0
    return pl.pallas_call(
        functools.partial(_ffn_prompt_kernel, seq_len=seq_len),
        grid=(n_tiles, nj),
        in_specs=[
            pl.BlockSpec((tm, d), lambda i, j: (i, 0)),
            pl.BlockSpec((1, d), lambda i, j: (0, 0)),
            pl.BlockSpec((None, d, tf), lambda i, j: (layer, 0, j)),
            pl.BlockSpec((None, d, tf), lambda i, j: (layer, 0, j + nj)),
            pl.BlockSpec((CONV_WIDTH, tf), lambda i, j: (0, j)),
            pl.BlockSpec((CONV_WIDTH, tf), lambda i, j: (0, j + nj)),
            pl.BlockSpec((1, tf), lambda i, j: (0, j)),
            pl.BlockSpec((1, tf), lambda i, j: (0, j + nj)),
            pl.BlockSpec((None, tf, d), lambda i, j: (layer, j, 0)),
            pl.BlockSpec((1, d), lambda i, j: (0, 0)),
        ],
        out_specs=[
            pl.BlockSpec((tm, d), lambda i, j: (i, 0)),
            pl.BlockSpec((1, CONV_HALO, tf), lambda i, j: (i, 0, j)),
            pl.BlockSpec((1, CONV_HALO, tf), lambda i, j: (i, 0, j)),
        ],
        out_shape=[jax.ShapeDtypeStruct((m, d), F32),
                   jax.ShapeDtypeStruct((n_tiles, CONV_HALO, d_ff), F32),
                   jax.ShapeDtypeStruct((n_tiles, CONV_HALO, d_ff), F32)],
        scratch_shapes=[pltpu.VMEM((tm, d), BF16), pltpu.VMEM((tm, d), F32),
                        pltpu.VMEM((CONV_HALO + tm, tf), F32), pltpu.VMEM((CONV_HALO + tm, tf), F32),
                        pltpu.VMEM((nj, CONV_HALO, tf), F32), pltpu.VMEM((nj, CONV_HALO, tf), F32)],
        compiler_params=_params("arbitrary", "arbitrary"),
        name="ffn_prompt",
    )(x, gpre, wup, wup, cw, cw, cb, cb, wd, gpost)


def _idx_sample_kernel(pt_ref, iq_ref, iw_ref, new_ref, *refs):
    del pt_ref
    page_refs, o_ref, onew_ref = refs[:-2], refs[-2], refs[-1]
    iq = iq_ref[0]
    iw = iw_ref[0] * (IDX_DIM ** -0.5 * N_IDX_HEADS ** -0.5)

    def page_scores(keys_t):
        s = _dot(iq, keys_t.astype(BF16))
        return jnp.sum(jnp.maximum(s, 0.0) * iw, axis=0, keepdims=True)

    o_ref[0] = jnp.concatenate([page_scores(r[0, 0]) for r in page_refs], axis=0)
    onew_ref[0] = page_scores(new_ref[0])


def _page_specs(layer, block, n_pages):
    def spec(p):
        def index(b, c, pt):
            return (layer, pt[b * n_pages + c * PAGES_PER_STEP + p]) + (0,) * (len(block) - 2)
        return pl.BlockSpec(block, index)
    return [spec(p) for p in range(PAGES_PER_STEP)]


def _idx_sample(pt_flat, iq3, iw3, new_page, cache_kidx, layer, n_pages):
    db = iq3.shape[0]
    steps = n_pages // PAGES_PER_STEP
    grid_spec = pltpu.PrefetchScalarGridSpec(
        num_scalar_prefetch=1,
        grid=(db, steps),
        in_specs=[pl.BlockSpec((1, N_IDX_HEADS, IDX_DIM), lambda b, c, pt: (b, 0, 0)),
                  pl.BlockSpec((1, N_IDX_HEADS, 1), lambda b, c, pt: (b, 0, 0)),
                  pl.BlockSpec((1, IDX_DIM, PAGE_SIZE), lambda b, c, pt: (b, 0, 0))]
                 + _page_specs(layer, (1, 1, IDX_DIM, PAGE_SIZE), n_pages),
        out_specs=[pl.BlockSpec((1, PAGES_PER_STEP, PAGE_SIZE), lambda b, c, pt: (b, c, 0)),
                   pl.BlockSpec((1, 1, PAGE_SIZE), lambda b, c, pt: (b, 0, 0))],
    )
    return pl.pallas_call(
        _idx_sample_kernel,
        grid_spec=grid_spec,
        out_shape=[jax.ShapeDtypeStruct((db, n_pages, PAGE_SIZE), F32),
                   jax.ShapeDtypeStruct((db, 1, PAGE_SIZE), F32)],
        compiler_params=_params("arbitrary", "arbitrary"),
        name="idx_sample",
    )(pt_flat, iq3, iw3, new_page, *([cache_kidx] * PAGES_PER_STEP))


def _select_sample_kernel(sc_ref, scn_ref, bias_ref, biasn_ref, key_scr, *, topk):
    db = sc_ref.shape[0]
    key_new = _sort_key(scn_ref[:, 0:1])
    key_scr[...] = _sort_key(sc_ref[...])

    def count_ge(t):
        past = jnp.sum(jnp.where(key_scr[...] >= t, 1.0, 0.0), axis=1, keepdims=True)
        return past + jnp.where(key_new >= t, 1.0, 0.0)

    thr = _kth_largest_key(count_ge, (db, 1), topk)
    bias_ref[...] = jnp.where(key_scr[...] >= thr, 0.0, -jnp.inf)
    biasn_ref[...] = jnp.where(key_new >= thr, 0.0, -jnp.inf)


def _select_sample(scores, scores_new, topk):
    db, past = scores.shape
    return pl.pallas_call(
        functools.partial(_select_sample_kernel, topk=topk),
        out_shape=[jax.ShapeDtypeStruct((db, past), F32), jax.ShapeDtypeStruct((db, 1), F32)],
        scratch_shapes=[pltpu.VMEM((db, past), jnp.int32)],
        compiler_params=pltpu.CompilerParams(vmem_limit_bytes=V7X_VMEM_LIMIT_BYTES),
        name="select_sample",
    )(scores, scores_new)


def _attn_sample_kernel(pt_ref, q_ref, bias_ref, biasn_ref, kn_ref, vn_ref, *refs):
    del pt_ref
    k_refs = refs[:PAGES_PER_STEP]
    v_refs = refs[PAGES_PER_STEP:2 * PAGES_PER_STEP]
    o_ref, m_scr, l_scr, acc_scr = refs[2 * PAGES_PER_STEP:]
    c = pl.program_id(1)
    scale = HEAD_DIM ** -0.5

    @pl.when(c == 0)
    def _():
        m_scr[...] = jnp.full(m_scr.shape, -jnp.inf, F32)
        l_scr[...] = jnp.zeros(l_scr.shape, F32)
        acc_scr[...] = jnp.zeros(acc_scr.shape, F32)

    def update(g, s, v):
        m_old = m_scr[g]
        m_new = jnp.maximum(m_old, jnp.max(s, axis=-1, keepdims=True))
        safe = jnp.where(m_new == -jnp.inf, 0.0, m_new)
        alpha = jnp.exp(m_old - safe)
        p = jnp.exp(s - safe)
        l_scr[g] = alpha * l_scr[g] + jnp.sum(p, axis=-1, keepdims=True)
        acc_scr[g] = alpha * acc_scr[g] + _dot(p.astype(BF16), v)
        m_scr[g] = m_new

    bias = jnp.concatenate([bias_ref[0, p:p + 1, :] for p in range(PAGES_PER_STEP)], axis=1)
    for g in range(N_KV_HEADS):
        kg = jnp.concatenate([r[0, 0, pl.ds(g, PAGE_SIZE, stride=N_KV_HEADS), :] for r in k_refs], axis=0)
        vg = jnp.concatenate([r[0, 0, pl.ds(g, PAGE_SIZE, stride=N_KV_HEADS), :] for r in v_refs], axis=0)
        s = _dot_t(q_ref[0, g], kg.astype(BF16)) * scale + bias
        update(g, s, vg.astype(BF16))

    @pl.when(c == pl.num_programs(1) - 1)
    def _():
        for g in range(N_KV_HEADS):
            kn = kn_ref[0, :, g * HEAD_DIM:(g + 1) * HEAD_DIM].astype(BF16)
            vn = vn_ref[0, :, g * HEAD_DIM:(g + 1) * HEAD_DIM].astype(BF16)
            qf = q_ref[0, g].astype(F32)
            s = jnp.sum(qf * kn.astype(F32), axis=-1, keepdims=True) * scale + biasn_ref[0]
            m_old = m_scr[g]
            m_new = jnp.maximum(m_old, s)
            alpha = jnp.exp(m_old - m_new)
            p = jnp.exp(s - m_new)
            l = alpha * l_scr[g] + p
            acc = alpha * acc_scr[g] + p.astype(BF16).astype(F32) * vn.astype(F32)
            o_ref[0, g] = acc / l


def _attn_sample(pt_flat, q4, bias3, bias_new, k_new, v_new, cache_k, cache_v, layer, n_pages):
    db, _, rows, _ = q4.shape
    steps = n_pages // PAGES_PER_STEP
    page_block = (1, 1, PAGE_SIZE * N_KV_HEADS, HEAD_DIM)
    grid_spec = pltpu.PrefetchScalarGridSpec(
        num_scalar_prefetch=1,
        grid=(db, steps),
        in_specs=[pl.BlockSpec((1, N_KV_HEADS, rows, HEAD_DIM), lambda b, c, pt: (b, 0, 0, 0)),
                  pl.BlockSpec((1, PAGES_PER_STEP, PAGE_SIZE), lambda b, c, pt: (b, c, 0)),
                  pl.BlockSpec((1, 1, 1), lambda b, c, pt: (b, 0, 0)),
                  pl.BlockSpec((1, 1, KV_WIDTH), lambda b, c, pt: (b, 0, 0)),
                  pl.BlockSpec((1, 1, KV_WIDTH), lambda b, c, pt: (b, 0, 0))]
                 + _page_specs(layer, page_block, n_pages) + _page_specs(layer, page_block, n_pages),
        out_specs=pl.BlockSpec((1, N_KV_HEADS, rows, HEAD_DIM), lambda b, c, pt: (b, 0, 0, 0)),
        scratch_shapes=[pltpu.VMEM((N_KV_HEADS, rows, 1), F32),
                        pltpu.VMEM((N_KV_HEADS, rows, 1), F32),
                        pltpu.VMEM((N_KV_HEADS, rows, HEAD_DIM), F32)],
    )
    return pl.pallas_call(
        _attn_sample_kernel,
        grid_spec=grid_spec,
        out_shape=jax.ShapeDtypeStruct((db, N_KV_HEADS, rows, HEAD_DIM), F32),
        compiler_params=_params("arbitrary", "arbitrary"),
        name="attn_sample",
    )(pt_flat, q4, bias3, bias_new, k_new, v_new, *([cache_k] * PAGES_PER_STEP), *([cache_v] * PAGES_PER_STEP))


def _mix_sample_kernel(x_ref, a_ref, e_ref, wo_ref, pw_ref, ps_ref, g_ref, y_ref):
    group = pw_ref.shape[1]
    n_rows = e_ref.shape[0]
    d_parts = []
    for g, w in enumerate(POOL_WINDOWS):
        cols = slice(g * group, (g + 1) * group)
        ug = e_ref[n_rows - 1, :, cols]
        wsum = ug
        for s in range(1, w):
            wsum = wsum + e_ref[n_rows - 1 - s, :, cols]
        d_parts.append(wsum / float(w) - ug)
    p = _pool_project(d_parts, pw_ref, ps_ref)
    o = _dot(a_ref[...].astype(BF16), wo_ref[0:ATTN_WIDTH, :]) + _dot(p, wo_ref[ATTN_WIDTH:, :])
    y_ref[...] = x_ref[...] + _rms(o, g_ref[...])


def _mix_sample(x, a, e, wo, pw, ps, g):
    return pl.pallas_call(
        _mix_sample_kernel,
        out_shape=jax.ShapeDtypeStruct(x.shape, F32),
        compiler_params=pltpu.CompilerParams(vmem_limit_bytes=V7X_VMEM_LIMIT_BYTES),
        name="mix_sample",
    )(x, a, e, wo, pw, ps, g)


def _ffn_sample_kernel(x_ref, gpre_ref, wg_ref, wv_ref, cwg_ref, cwv_ref, cbg_ref, cbv_ref, sg_ref, sv_ref,
                       wd_ref, gpost_ref, y_ref, hg_ref, hv_ref, xn_scr, acc_scr):
    j = pl.program_id(0)

    @pl.when(j == 0)
    def _():
        xn_scr[...] = _rms(x_ref[...], gpre_ref[...]).astype(BF16)
        acc_scr[...] = jnp.zeros(acc_scr.shape, F32)

    xn = xn_scr[...]
    hg = _dot(xn, wg_ref[...])
    hv = _dot(xn, wv_ref[...])
    hg_ref[...] = hg
    hv_ref[...] = hv

    def conv(h, s_ref, cw_ref, cb_ref):
        return cb_ref[...] + cw_ref[0:1, :] * s_ref[0] + cw_ref[1:2, :] * s_ref[1] + cw_ref[2:3, :] * h

    cg = conv(hg, sg_ref, cwg_ref, cbg_ref)
    cv = conv(hv, sv_ref, cwv_ref, cbv_ref)
    act = (cg * jax.nn.sigmoid(cg) * cv).astype(BF16)
    acc_scr[...] += _dot(act, wd_ref[...])

    @pl.when(j == pl.num_programs(0) - 1)
    def _():
        y_ref[...] = x_ref[...] + _rms(acc_scr[...], gpost_ref[...])


def _ffn_sample(x, gpre, wup, cw, cb, state, wd, gpost, layer, tf):
    db, d = x.shape
    d_ff = wd.shape[1]
    nj = d_ff // tf
    n_state = state.shape[0]
    return pl.pallas_call(
        _ffn_sample_kernel,
        grid=(nj,),
        in_specs=[
            pl.BlockSpec((db, d), lambda j: (0, 0)),
            pl.BlockSpec((1, d), lambda j: (0, 0)),
            pl.BlockSpec((None, d, tf), lambda j: (layer, 0, j)),
            pl.BlockSpec((None, d, tf), lambda j: (layer, 0, j + nj)),
            pl.BlockSpec((CONV_WIDTH, tf), lambda j: (0, j)),
            pl.BlockSpec((CONV_WIDTH, tf), lambda j: (0, j + nj)),
            pl.BlockSpec((1, tf), lambda j: (0, j)),
            pl.BlockSpec((1, tf), lambda j: (0, j + nj)),
            pl.BlockSpec((n_state, db, tf), lambda j: (0, 0, j)),
            pl.BlockSpec((n_state, db, tf), lambda j: (0, 0, j + nj)),
            pl.BlockSpec((None, tf, d), lambda j: (layer, j, 0)),
            pl.BlockSpec((1, d), lambda j: (0, 0)),
        ],
        out_specs=[pl.BlockSpec((db, d), lambda j: (0, 0)),
                   pl.BlockSpec((db, tf), lambda j: (0, j)),
                   pl.BlockSpec((db, tf), lambda j: (0, j))],
        out_shape=[jax.ShapeDtypeStruct((db, d), F32), jax.ShapeDtypeStruct((db, d_ff), F32),
                   jax.ShapeDtypeStruct((db, d_ff), F32)],
        scratch_shapes=[pltpu.VMEM((db, d), BF16), pltpu.VMEM((db, d), F32)],
        compiler_params=_params("arbitrary"),
        name="ffn_sample",
    )(x, gpre, wup, wup, cw, cw, cb, cb, state, state, wd, gpost)


def _split_w_in(w_in_l):
    d = w_in_l.shape[0]
    q_end = ATTN_WIDTH
    k_end = q_end + KV_WIDTH
    v_end = k_end + KV_WIDTH
    iq_end = v_end + N_IDX_HEADS * IDX_DIM
    ik_end = iq_end + IDX_DIM
    iw_end = ik_end + N_IDX_HEADS
    w = w_in_l.astype(BF16)
    w_ik = w[:, iq_end:ik_end]
    zeros = lambda c: jnp.zeros((d, c), BF16)
    idx = jnp.concatenate([w_ik, zeros(V7X_LANES - IDX_DIM), zeros(V7X_LANES - IDX_DIM), w_ik,
                           w[:, ik_end:iw_end], zeros(V7X_LANES - N_IDX_HEADS)], axis=1)
    return {"q": w[:, :q_end], "kv": w[:, q_end:v_end], "iq": w[:, v_end:iq_end], "idx": idx, "u": w[:, iw_end:]}


def kernel(x_prompt, x_sample, cache_k, cache_v, cache_kidx, state_pool, state_conv, page_table, meta_tokens,
           w_in, w_o, pool_w, pool_scale, g_mix_pre, g_mix_post, g_ffn_pre, g_ffn_post, w_up, conv_w, conv_b,
           w_down):
    b, seq, d = x_prompt.shape
    db, ds, _ = x_sample.shape
    assert ds == 1, "the sample kernels handle one new token per sequence"
    depth = w_in.shape[0]
    n_pool = cache_k.shape[1]
    n_pages = page_table.shape[1]
    assert cache_k.shape[2] == PAGE_SIZE and n_pages % PAGES_PER_STEP == 0
    past = n_pages * PAGE_SIZE
    t_len = seq + N_META
    d_ff = w_down.shape[1]
    pool_width = d - ATTN_WIDTH
    topk_p = min(TOPK_MAX, t_len // 4)
    topk_s = min(TOPK_MAX, (past + ds) // 4)
    assert past >= max(POOL_WINDOWS)

    m_rows = b * t_len
    tm_proj = _row_tile(m_rows, min(TM_PROJ_CAP, t_len))
    tm_ffn = _row_tile(m_rows, min(TM_FFN_CAP, t_len))
    tf = _ff_tile(d_ff)
    sub8 = V7X_SUBLANES_F32
    heads_pad = -(-HEADS_PER_KV // sub8) * sub8

    meta = jnp.broadcast_to(meta_tokens[None].astype(x_prompt.dtype), (b, N_META, d))
    xp = jnp.concatenate([meta, x_prompt], axis=1).reshape(b * t_len, d)
    xs = x_sample.reshape(db, d)
    pt_flat = page_table.reshape(-1).astype(jnp.int32)
    ck = cache_k.reshape(depth, n_pool, PAGE_SIZE * N_KV_HEADS, HEAD_DIM)
    cv = cache_v.reshape(depth, n_pool, PAGE_SIZE * N_KV_HEADS, HEAD_DIM)
    kidx_t = jnp.swapaxes(cache_kidx, 2, 3)

    row2 = lambda a: a.reshape(1, -1)
    outs = {name: [] for name in ("kp", "vp", "ikp", "poolp", "convp", "ks", "vs", "iks", "pools", "convs")}
    wup = w_up.astype(BF16)
    wd = w_down.astype(BF16)
    for l in range(depth):
        w = _split_w_in(w_in[l])
        wo = w_o[l].astype(BF16)
        pw = pool_w[l].astype(BF16)
        ps = row2(pool_scale[l])
        cw = conv_w[l]
        cb = row2(conv_b[l])

        q, k, v, kb, vb, iq, ik, ik2, iw, u = _in_proj(xp, row2(g_mix_pre[l]), w, tm_proj)
        b3 = lambda a: a.reshape(b, t_len, a.shape[-1])
        a = _attn_prompt(b3(q), b3(iq), jnp.swapaxes(b3(iw), 1, 2), b3(kb), b3(vb), b3(ik2), topk_p)
        xp = _mix_prompt(xp, a.reshape(m_rows, ATTN_WIDTH), u, wo, pw, ps, row2(g_mix_post[l]), tm_proj, t_len)
        xp, last_g, last_v = _ffn_prompt(xp, row2(g_ffn_pre[l]), wup, cw, cb, wd, row2(g_ffn_post[l]), l,
                                         tm_ffn, tf, t_len)
        outs["kp"].append(k.reshape(b, t_len, N_KV_HEADS, HEAD_DIM))
        outs["vp"].append(v.reshape(b, t_len, N_KV_HEADS, HEAD_DIM))
        outs["ikp"].append(ik.reshape(b, t_len, IDX_DIM))
        outs["poolp"].append(u.reshape(b, t_len, pool_width)[:, t_len - POOL_BUF:])
        end_tiles = [((s + 1) * t_len - 1) // tm_ffn for s in range(b)]
        seq_last = lambda h: jnp.stack([h[i, CONV_HALO - (CONV_WIDTH - 1):] for i in end_tiles])
        outs["convp"].append(jnp.concatenate([seq_last(last_g), seq_last(last_v)], axis=-1))

        q, k, v, _, _, iq, ik, _, iw, u = _in_proj(xs, row2(g_mix_pre[l]), w, db)
        iq3 = iq.reshape(db, N_IDX_HEADS, IDX_DIM)
        new_page = jnp.pad(ik[:, :, None], ((0, 0), (0, 0), (0, PAGE_SIZE - 1)))
        scores, scores_new = _idx_sample(pt_flat, iq3, iw.reshape(db, N_IDX_HEADS, 1), new_page, kidx_t, l, n_pages)
        bias, bias_new = _select_sample(scores.reshape(db, past), scores_new.reshape(db, PAGE_SIZE), topk_s)
        q4 = jnp.pad(q.reshape(db, N_KV_HEADS, HEADS_PER_KV, HEAD_DIM),
                     ((0, 0), (0, 0), (0, heads_pad - HEADS_PER_KV), (0, 0)))
        a = _attn_sample(pt_flat, q4, bias.reshape(db, n_pages, PAGE_SIZE), bias_new.reshape(db, 1, 1),
                         k.reshape(db, 1, KV_WIDTH), v.reshape(db, 1, KV_WIDTH), ck, cv, l, n_pages)
        a = a[:, :, :HEADS_PER_KV].reshape(db, ATTN_WIDTH)
        e = jnp.concatenate([jnp.swapaxes(state_pool[l], 0, 1).astype(u.dtype), u[None]], axis=0)
        xs = _mix_sample(xs, a, e, wo, pw, ps, row2(g_mix_post[l]))
        xs, hg, hv = _ffn_sample(xs, row2(g_ffn_pre[l]), wup, cw, cb, jnp.swapaxes(state_conv[l], 0, 1), wd,
                                 row2(g_ffn_post[l]), l, tf)
        outs["ks"].append(k.reshape(db, ds, N_KV_HEADS, HEAD_DIM))
        outs["vs"].append(v.reshape(db, ds, N_KV_HEADS, HEAD_DIM))
        outs["iks"].append(ik.reshape(db, ds, IDX_DIM))
        outs["pools"].append(jnp.concatenate([state_pool[l].astype(u.dtype), u[:, None, :]], axis=1)[:, 1:])
        h = jnp.concatenate([hg, hv], axis=-1)[:, None, :]
        outs["convs"].append(jnp.concatenate([state_conv[l].astype(h.dtype), h], axis=1)[:, -(CONV_WIDTH - 1):])

    y_prompt = xp.reshape(b, t_len, d)[:, N_META:]
    y_sample = xs.reshape(db, ds, d)
    stack = lambda name: jnp.stack(outs[name])
    return (y_prompt, y_sample, stack("kp"), stack("vp"), stack("ikp"), stack("poolp"), stack("convp"),
            stack("ks"), stack("vs"), stack("iks"), stack("pools"), stack("convs"))
```

```python
import functools

import jax
import jax.numpy as jnp
from jax import lax
from jax.experimental import pallas as pl
from jax.experimental.pallas import tpu as pltpu

N_META = 16
N_HEADS = 8
N_KV_HEADS = 2
HEAD_DIM = 128
N_IDX_HEADS = 16
IDX_DIM = 64
TOPK_MAX = 256
POOL_WINDOWS = (2, 4, 8, 16)
CONV_WIDTH = 3
PAGE_SIZE = 128
Q_BLOCK = 128
RMS_EPS = 1e-6

ATTN_WIDTH = N_HEADS * HEAD_DIM
KV_WIDTH = N_KV_HEADS * HEAD_DIM
HEADS_PER_KV = N_HEADS // N_KV_HEADS
POOL_BUF = max(POOL_WINDOWS) - 1

V7X_LANES = 128
V7X_SUBLANES_F32 = 8
V7X_SUBLANES_BF16 = 16
V7X_VMEM_LIMIT_BYTES = 56 * 1024 * 1024

INT_MIN = -(2 ** 31)
F32 = jnp.float32
BF16 = jnp.bfloat16

CONV_HALO = V7X_SUBLANES_F32
POOL_HALO = 2 * V7X_SUBLANES_F32
IDX_PAGES_PER_STEP = 64
ATTN_PAGES_PER_STEP = 32
KEY_CHUNK = 256
TM_PROJ_CAP = 384
TM_FFN_CAP = 688


def _params(*sem):
    return pltpu.CompilerParams(dimension_semantics=sem, vmem_limit_bytes=V7X_VMEM_LIMIT_BYTES)


def _row_tile(seq_len, cap):
    best = None
    for d in range(V7X_SUBLANES_BF16, min(seq_len, cap) + 1, V7X_SUBLANES_BF16):
        if seq_len % d == 0:
            best = d
    assert best is not None, seq_len
    return best


def _ff_tile(d_ff, cap=512):
    best = None
    for d in range(V7X_LANES, min(d_ff, cap) + 1, V7X_LANES):
        if d_ff % d == 0:
            best = d
    assert best is not None, d_ff
    return best


def _rms(x, g):
    return x * lax.rsqrt(jnp.mean(x * x, axis=-1, keepdims=True) + RMS_EPS) * g


def _dot(a, b):
    return jnp.dot(a, b, preferred_element_type=F32)


def _dot_t(a, b):
    return lax.dot_general(a, b, (((1,), (1,)), ((), ())), preferred_element_type=F32)


def _const_spec(shape):
    n = len(shape)
    return pl.BlockSpec(shape, lambda *_: (0,) * n)


def _in_proj_kernel(x_ref, g_ref, wq_ref, wkv_ref, wiq_ref, widx_ref, wu_ref,
                    q_ref, k_ref, v_ref, kb_ref, vb_ref, iq_ref, ik_ref, ik2_ref, iw_ref, u_ref):
    hn = _rms(x_ref[...], g_ref[...]).astype(BF16)
    q_ref[...] = _dot(hn, wq_ref[...]).astype(BF16)
    kv = _dot(hn, wkv_ref[...])
    tm = x_ref.shape[0]
    for out_ref, c0 in ((k_ref, 0), (v_ref, KV_WIDTH)):
        for g in range(N_KV_HEADS):
            out_ref[pl.ds(g, tm, stride=N_KV_HEADS), :] = kv[:, c0 + g * HEAD_DIM:c0 + (g + 1) * HEAD_DIM]
    kb_ref[...] = kv[:, :KV_WIDTH].astype(BF16)
    vb_ref[...] = kv[:, KV_WIDTH:].astype(BF16)
    iq_ref[...] = _dot(hn, wiq_ref[...]).astype(BF16)
    idx = _dot(hn, widx_ref[...])
    ik_ref[...] = idx[:, :IDX_DIM]
    ik2_ref[...] = idx[:, :2 * V7X_LANES].astype(BF16)
    iw_ref[...] = idx[:, 2 * V7X_LANES:2 * V7X_LANES + N_IDX_HEADS]
    u_ref[...] = _dot(hn, wu_ref[...])


def _in_proj(x, g, w, tm):
    m, d = x.shape
    pool_width = w["u"].shape[1]
    row = lambda c: pl.BlockSpec((tm, c), lambda i: (i, 0))
    outs = [
        (1, ATTN_WIDTH, BF16), (N_KV_HEADS, HEAD_DIM, F32), (N_KV_HEADS, HEAD_DIM, F32), (1, KV_WIDTH, BF16),
        (1, KV_WIDTH, BF16), (1, N_IDX_HEADS * IDX_DIM, BF16), (1, IDX_DIM, F32), (1, 2 * V7X_LANES, BF16),
        (1, N_IDX_HEADS, F32), (1, pool_width, F32),
    ]
    return pl.pallas_call(
        _in_proj_kernel,
        grid=(m // tm,),
        in_specs=[row(d), _const_spec((1, d)), _const_spec(w["q"].shape), _const_spec(w["kv"].shape),
                  _const_spec(w["iq"].shape), _const_spec(w["idx"].shape), _const_spec(w["u"].shape)],
        out_specs=[pl.BlockSpec((r * tm, c), lambda i: (i, 0)) for r, c, _ in outs],
        out_shape=[jax.ShapeDtypeStruct((r * m, c), dt) for r, c, dt in outs],
        compiler_params=_params("arbitrary"),
        name="in_proj",
    )(x, g, w["q"], w["kv"], w["iq"], w["idx"], w["u"])


def _sort_key(x):
    bits = lax.bitcast_convert_type(x, jnp.int32)
    return bits ^ (lax.shift_right_arithmetic(bits, 31) & jnp.int32(0x7FFFFFFF))


def _kth_largest_key(count_ge, shape, k):
    def body(it, t):
        cand = t | lax.shift_left(jnp.int32(1), 31 - it)
        cnt = count_ge(cand ^ jnp.int32(INT_MIN))
        return jnp.where(cnt >= k, cand, t)

    t = lax.fori_loop(0, 32, body, jnp.zeros(shape, jnp.int32))
    return t ^ jnp.int32(INT_MIN)


def _attn_prompt_kernel(q_ref, iq_ref, iwt_ref, kb_ref, vb_ref, ik2_ref, o_ref,
                        k_scr, v_scr, ik_scr, key_scr, thr_scr, bias_scr, s_scr, m_scr, l_scr, acc_scr, *, topk):
    i = pl.program_id(1)
    t_len = kb_ref.shape[1]
    n_chunks, ch, _ = key_scr.shape
    tk = n_chunks * ch
    halves = ch // V7X_LANES
    rows = HEADS_PER_KV * Q_BLOCK
    nc = ((i + 1) * Q_BLOCK + ch - 1) // ch

    @pl.when(i == 0)
    def _():
        k_scr[0:t_len, :] = kb_ref[0]
        v_scr[0:t_len, :] = vb_ref[0]
        ik_scr[0, 0:t_len, :] = ik2_ref[0, :, 0:V7X_LANES]
        ik_scr[1, 0:t_len, :] = ik2_ref[0, :, V7X_LANES:]
        if tk > t_len:
            k_scr[t_len:tk, :] = jnp.zeros((tk - t_len, k_scr.shape[1]), BF16)
            v_scr[t_len:tk, :] = jnp.zeros((tk - t_len, v_scr.shape[1]), BF16)
            ik_scr[0, t_len:tk, :] = jnp.zeros((tk - t_len, V7X_LANES), BF16)
            ik_scr[1, t_len:tk, :] = jnp.zeros((tk - t_len, V7X_LANES), BF16)

    iw = iwt_ref[0] * (IDX_DIM ** -0.5 * N_IDX_HEADS ** -0.5)
    iq2 = [jnp.concatenate([iq_ref[0, :, (2 * jj + e) * V7X_LANES:(2 * jj + e + 1) * V7X_LANES] for e in range(2)],
                           axis=0) for jj in range(N_IDX_HEADS // 4)]
    qpos = i * Q_BLOCK + lax.broadcasted_iota(jnp.int32, (1, Q_BLOCK), 1)

    def over_chunks(chunk_fn):
        def pair(p, carry):
            chunk_fn(2 * p)
            chunk_fn(2 * p + 1)
            return carry

        lax.fori_loop(0, lax.shift_right_logical(nc, 1), pair, 0)

        @pl.when((nc & 1) == 1)
        def _():
            chunk_fn(nc - 1)

    def index_chunk(c):
        r0 = pl.multiple_of(c * ch, ch)
        keys = jnp.concatenate([ik_scr[0, pl.ds(r0, ch), :], ik_scr[1, pl.ds(r0, ch), :]], axis=0)
        acc = jnp.zeros((ch, Q_BLOCK), F32)
        for jj in range(N_IDX_HEADS // 4):
            s = _dot_t(keys, iq2[jj])
            for e in range(2):
                for half in range(2):
                    h = 2 * (2 * jj + e) + half
                    blk = s[half * ch:(half + 1) * ch, e * Q_BLOCK:(e + 1) * Q_BLOCK]
                    acc = acc + jnp.maximum(blk, 0.0) * iw[h:h + 1, :]
        kpos = r0 + lax.broadcasted_iota(jnp.int32, (ch, 1), 0)
        key_scr[c] = jnp.where(kpos <= qpos, _sort_key(acc), jnp.int32(INT_MIN))

    over_chunks(index_chunk)

    for n_static in range(1, n_chunks + 1):
        @pl.when(nc == n_static)
        def _(n_static=n_static):
            def count_ge(t):
                part = jnp.zeros((V7X_SUBLANES_F32, Q_BLOCK), F32)
                for c in range(n_static):
                    hit = jnp.where(key_scr[c] >= t, 1.0, 0.0)
                    part = part + jnp.sum(hit.reshape(ch // V7X_SUBLANES_F32, V7X_SUBLANES_F32, Q_BLOCK), axis=0)
                return jnp.sum(part, axis=0, keepdims=True)

            t = _kth_largest_key(count_ge, (1, Q_BLOCK), topk)
            thr_scr[...] = jnp.broadcast_to(jnp.maximum(t, jnp.int32(INT_MIN + 1)), thr_scr.shape)

    thr = thr_scr[0:1, :]

    def bias_chunk(c):
        bias_t = jnp.where(key_scr[c] >= thr, 0.0, -jnp.inf)
        for r in range(halves):
            bias_scr[c, :, r * V7X_LANES:(r + 1) * V7X_LANES] = bias_t[r * V7X_LANES:(r + 1) * V7X_LANES, :].T

    over_chunks(bias_chunk)

    scale = HEAD_DIM ** -0.5
    gcols = [slice(g * HEAD_DIM, (g + 1) * HEAD_DIM) for g in range(N_KV_HEADS)]
    qgs = [jnp.concatenate([q_ref[0, :, (g * HEADS_PER_KV + r) * HEAD_DIM:(g * HEADS_PER_KV + r + 1) * HEAD_DIM]
                            for r in range(HEADS_PER_KV)], axis=0) for g in range(N_KV_HEADS)]
    lane_tiles = lambda x: [x[:, r * V7X_LANES:(r + 1) * V7X_LANES] for r in range(halves)]

    m_scr[...] = jnp.full(m_scr.shape, -jnp.inf, F32)

    def score_chunk(c):
        r0 = pl.multiple_of(c * ch, ch)
        bias = bias_scr[c][None]
        for g in range(N_KV_HEADS):
            s = _dot_t(qgs[g], k_scr[pl.ds(r0, ch), gcols[g]])
            s = (s.reshape(HEADS_PER_KV, Q_BLOCK, ch) * scale + bias).reshape(rows, ch)
            s_scr[g, c] = s
            m = m_scr[g]
            for tile in lane_tiles(s):
                m = jnp.maximum(m, tile)
            m_scr[g] = m

    over_chunks(score_chunk)
    for g in range(N_KV_HEADS):
        m_scr[g] = jnp.broadcast_to(jnp.max(m_scr[g], axis=1, keepdims=True), m_scr.shape[1:])
    l_scr[...] = jnp.zeros(l_scr.shape, F32)
    acc_scr[...] = jnp.zeros(acc_scr.shape, F32)

    def value_chunk(c):
        r0 = pl.multiple_of(c * ch, ch)
        for g in range(N_KV_HEADS):
            m = m_scr[g]
            p_tiles = [jnp.exp(tile - m) for tile in lane_tiles(s_scr[g, c])]
            l = l_scr[g]
            for tile in p_tiles:
                l = l + tile
            l_scr[g] = l
            p = jnp.concatenate(p_tiles, axis=1).astype(BF16)
            acc_scr[g] += _dot(p, v_scr[pl.ds(r0, ch), gcols[g]])

    over_chunks(value_chunk)
    for g in range(N_KV_HEADS):
        o = acc_scr[g] / jnp.sum(l_scr[g], axis=1, keepdims=True)
        for r in range(HEADS_PER_KV):
            h = g * HEADS_PER_KV + r
            o_ref[0, :, h * HEAD_DIM:(h + 1) * HEAD_DIM] = o[r * Q_BLOCK:(r + 1) * Q_BLOCK].astype(o_ref.dtype)


def _attn_prompt(q, iq, iwt, kb, vb, ik2, topk):
    b, t_len, _ = q.shape
    n_blk = pl.cdiv(t_len, Q_BLOCK)
    n_chunks = pl.cdiv(n_blk * Q_BLOCK, KEY_CHUNK)
    tk = n_chunks * KEY_CHUNK
    rows = HEADS_PER_KV * Q_BLOCK
    qblk = lambda c: pl.BlockSpec((1, Q_BLOCK, c), lambda bi, i: (bi, i, 0))
    full = lambda c: pl.BlockSpec((1, t_len, c), lambda bi, i: (bi, 0, 0))
    return pl.pallas_call(
        functools.partial(_attn_prompt_kernel, topk=topk),
        grid=(b, n_blk),
        in_specs=[qblk(ATTN_WIDTH), qblk(N_IDX_HEADS * IDX_DIM),
                  pl.BlockSpec((1, N_IDX_HEADS, Q_BLOCK), lambda bi, i: (bi, 0, i)),
                  full(KV_WIDTH), full(KV_WIDTH), full(2 * V7X_LANES)],
        out_specs=qblk(ATTN_WIDTH),
        out_shape=jax.ShapeDtypeStruct((b, t_len, ATTN_WIDTH), BF16),
        scratch_shapes=[pltpu.VMEM((tk, KV_WIDTH), BF16), pltpu.VMEM((tk, KV_WIDTH), BF16),
                        pltpu.VMEM((2, tk, V7X_LANES), BF16),
                        pltpu.VMEM((n_chunks, KEY_CHUNK, Q_BLOCK), jnp.int32),
                        pltpu.VMEM((V7X_SUBLANES_F32, Q_BLOCK), jnp.int32),
                        pltpu.VMEM((n_chunks, Q_BLOCK, KEY_CHUNK), F32),
                        pltpu.VMEM((N_KV_HEADS, n_chunks, rows, KEY_CHUNK), F32),
                        pltpu.VMEM((N_KV_HEADS, rows, V7X_LANES), F32),
                        pltpu.VMEM((N_KV_HEADS, rows, V7X_LANES), F32),
                        pltpu.VMEM((N_KV_HEADS, rows, HEAD_DIM), F32)],
        compiler_params=_params("arbitrary", "arbitrary"),
        name="attn_prompt",
    )(q, iq, iwt, kb, vb, ik2)


def _pool_project(d_parts, pw_ref, ps_ref):
    group = d_parts[0].shape[1]
    outs = []
    for g, d in enumerate(d_parts):
        outs.append(_dot(d.astype(BF16), pw_ref[g]) * ps_ref[:, g * group:(g + 1) * group])
    return jnp.concatenate(outs, axis=1).astype(BF16)


def _pos_in_seq(i, tm, seq_len):
    t = (i * tm) % seq_len + lax.broadcasted_iota(jnp.int32, (tm, 1), 0)
    return jnp.where(t >= seq_len, t - seq_len, t)


def _mix_prompt_kernel(x_ref, a_ref, u_ref, wo_ref, pw_ref, ps_ref, g_ref, y_ref, e_scr, *, seq_len):
    i = pl.program_id(0)
    tm = x_ref.shape[0]
    group = pw_ref.shape[1]

    @pl.when(i == 0)
    def _():
        e_scr[0:POOL_HALO, :] = jnp.zeros((POOL_HALO, e_scr.shape[1]), F32)

    u = u_ref[...]
    e_scr[POOL_HALO:POOL_HALO + tm, :] = u
    t = _pos_in_seq(i, tm, seq_len)
    in_seq = [None] + [(t >= s).astype(F32) for s in range(1, max(POOL_WINDOWS))]
    d_parts = []
    for g, w in enumerate(POOL_WINDOWS):
        cols = slice(g * group, (g + 1) * group)
        ug = u[:, cols]
        wsum = ug
        for s in range(1, w):
            wsum = wsum + in_seq[s] * e_scr[POOL_HALO - s:POOL_HALO - s + tm, cols]
        cnt = jnp.minimum(t + 1, w).astype(F32)
        d_parts.append(wsum / cnt - ug)
    e_scr[0:POOL_HALO, :] = u[tm - POOL_HALO:, :]
    p = _pool_project(d_parts, pw_ref, ps_ref)
    o = _dot(a_ref[...], wo_ref[0:ATTN_WIDTH, :]) + _dot(p, wo_ref[ATTN_WIDTH:, :])
    y_ref[...] = x_ref[...] + _rms(o, g_ref[...])


def _mix_prompt(x, a, u, wo, pw, ps, g, tm, seq_len):
    m, d = x.shape
    pool_width = u.shape[1]
    assert POOL_HALO <= tm <= seq_len
    row = lambda c: pl.BlockSpec((tm, c), lambda i: (i, 0))
    return pl.pallas_call(
        functools.partial(_mix_prompt_kernel, seq_len=seq_len),
        grid=(m // tm,),
        in_specs=[row(d), row(ATTN_WIDTH), row(pool_width), _const_spec(wo.shape), _const_spec(pw.shape),
                  _const_spec((1, pool_width)), _const_spec((1, d))],
        out_specs=row(d),
        out_shape=jax.ShapeDtypeStruct((m, d), F32),
        scratch_shapes=[pltpu.VMEM((POOL_HALO + tm, pool_width), F32)],
        compiler_params=_params("arbitrary"),
        name="mix_prompt",
    )(x, a, u, wo, pw, ps, g)


def _ffn_prompt_kernel(x_ref, gpre_ref, wg_ref, wv_ref, cwg_ref, cwv_ref, cbg_ref, cbv_ref, wd_ref, gpost_ref,
                       y_ref, lastg_ref, lastv_ref,
                       xn_scr, acc_scr, hg_scr, hv_scr, carryg_scr, carryv_scr, *, seq_len):
    i = pl.program_id(0)
    j = pl.program_id(1)
    tm = x_ref.shape[0]

    @pl.when(j == 0)
    def _():
        xn_scr[...] = _rms(x_ref[...], gpre_ref[...]).astype(BF16)
        acc_scr[...] = jnp.zeros(acc_scr.shape, F32)

    @pl.when(i == 0)
    def _():
        carryg_scr[j] = jnp.zeros(carryg_scr.shape[1:], F32)
        carryv_scr[j] = jnp.zeros(carryv_scr.shape[1:], F32)

    t = _pos_in_seq(i, tm, seq_len)
    in_seq1 = (t >= 1).astype(F32)
    in_seq2 = (t >= 2).astype(F32)
    seq_end = jnp.minimum(seq_len - (i * tm) % seq_len, tm)
    last_row0 = pl.multiple_of(seq_end, CONV_HALO)

    xn = xn_scr[...]
    acts = []
    for w_ref, cw_ref, cb_ref, h_scr, carry_scr, last_ref in (
            (wg_ref, cwg_ref, cbg_ref, hg_scr, carryg_scr, lastg_ref),
            (wv_ref, cwv_ref, cbv_ref, hv_scr, carryv_scr, lastv_ref)):
        h = _dot(xn, w_ref[...])
        h_scr[0:CONV_HALO, :] = carry_scr[j]
        h_scr[CONV_HALO:CONV_HALO + tm, :] = h
        carry_scr[j] = h[tm - CONV_HALO:, :]
        last_ref[0] = h_scr[pl.ds(last_row0, CONV_HALO), :]
        acts.append(cb_ref[...]
                    + cw_ref[0:1, :] * (in_seq2 * h_scr[CONV_HALO - 2:CONV_HALO - 2 + tm, :])
                    + cw_ref[1:2, :] * (in_seq1 * h_scr[CONV_HALO - 1:CONV_HALO - 1 + tm, :])
                    + cw_ref[2:3, :] * h)
    cg, cv = acts
    act = (cg * jax.nn.sigmoid(cg) * cv).astype(BF16)
    acc_scr[...] += _dot(act, wd_ref[...])

    @pl.when(j == pl.num_programs(1) - 1)
    def _():
        y_ref[...] = x_ref[...] + _rms(acc_scr[...], gpost_ref[...])


def _ffn_prompt(x, gpre, wup, cw, cb, wd, gpost, layer, tm, tf, seq_len):
    m, d = x.shape
    d_ff = wd.shape[1]
    nj = d_ff // tf
    n_tiles = m // tm
    assert CONV_HALO <= tm <= seq_len and seq_len % CONV_HALO == 0 and tm % CONV_HALO == 0
    return pl.pallas_call(
        functools.partial(_ffn_prompt_kernel, seq_len=seq_len),
        grid=(n_tiles, nj),
        in_specs=[
            pl.BlockSpec((tm, d), lambda i, j: (i, 0)),
            pl.BlockSpec((1, d), lambda i, j: (0, 0)),
            pl.BlockSpec((None, d, tf), lambda i, j: (layer, 0, j)),
            pl.BlockSpec((None, d, tf), lambda i, j: (layer, 0, j + nj)),
            pl.BlockSpec((CONV_WIDTH, tf), lambda i, j: (0, j)),
            pl.BlockSpec((CONV_WIDTH, tf), lambda i, j: (0, j + nj)),
            pl.BlockSpec((1, tf), lambda i, j: (0, j)),
            pl.BlockSpec((1, tf), lambda i, j: (0, j + nj)),
            pl.BlockSpec((None, tf, d), lambda i, j: (layer, j, 0)),
            pl.BlockSpec((1, d), lambda i, j: (0, 0)),
        ],
        out_specs=[
            pl.BlockSpec((tm, d), lambda i, j: (i, 0)),
            pl.BlockSpec((1, CONV_HALO, tf), lambda i, j: (i, 0, j)),
            pl.BlockSpec((1, CONV_HALO, tf), lambda i, j: (i, 0, j)),
        ],
        out_shape=[jax.ShapeDtypeStruct((m, d), F32),
                   jax.ShapeDtypeStruct((n_tiles, CONV_HALO, d_ff), F32),
                   jax.ShapeDtypeStruct((n_tiles, CONV_HALO, d_ff), F32)],
        scratch_shapes=[pltpu.VMEM((tm, d), BF16), pltpu.VMEM((tm, d), F32),
                        pltpu.VMEM((CONV_HALO + tm, tf), F32), pltpu.VMEM((CONV_HALO + tm, tf), F32),
                        pltpu.VMEM((nj, CONV_HALO, tf), F32), pltpu.VMEM((nj, CONV_HALO, tf), F32)],
        compiler_params=_params("arbitrary", "arbitrary"),
        name="ffn_prompt",
    )(x, gpre, wup, wup, cw, cw, cb, cb, wd, gpost)


def _idx_sample_kernel(pt_ref, iq_ref, iw_ref, new_ref, *refs):
    del pt_ref
    page_refs, o_ref, onew_ref = refs[:-2], refs[-2], refs[-1]
    iq = iq_ref[0]
    iw = iw_ref[0] * (IDX_DIM ** -0.5 * N_IDX_HEADS ** -0.5)

    def scores(keys_t):
        s = _dot(iq, keys_t.astype(BF16))
        return jnp.sum(jnp.maximum(s, 0.0) * iw, axis=0, keepdims=True)

    o_ref[0] = scores(jnp.concatenate([r[0, 0] for r in page_refs], axis=1))
    onew_ref[0] = scores(new_ref[0])


def _page_specs(layer, block, n_pages, per_step):
    def spec(p):
        def index(b, c, pt):
            return (layer, pt[b * n_pages + c * per_step + p]) + (0,) * (len(block) - 2)
        return pl.BlockSpec(block, index)
    return [spec(p) for p in range(per_step)]


def _idx_sample(pt_flat, iq3, iw3, new_page, cache_kidx, layer, n_pages):
    db = iq3.shape[0]
    per_step = min(n_pages, IDX_PAGES_PER_STEP)
    assert n_pages % per_step == 0
    grid_spec = pltpu.PrefetchScalarGridSpec(
        num_scalar_prefetch=1,
        grid=(db, n_pages // per_step),
        in_specs=[pl.BlockSpec((1, N_IDX_HEADS, IDX_DIM), lambda b, c, pt: (b, 0, 0)),
                  pl.BlockSpec((1, N_IDX_HEADS, 1), lambda b, c, pt: (b, 0, 0)),
                  pl.BlockSpec((1, IDX_DIM, PAGE_SIZE), lambda b, c, pt: (b, 0, 0))]
                 + _page_specs(layer, (1, 1, IDX_DIM, PAGE_SIZE), n_pages, per_step),
        out_specs=[pl.BlockSpec((1, 1, per_step * PAGE_SIZE), lambda b, c, pt: (b, 0, c)),
                   pl.BlockSpec((1, 1, PAGE_SIZE), lambda b, c, pt: (b, 0, 0))],
    )
    return pl.pallas_call(
        _idx_sample_kernel,
        grid_spec=grid_spec,
        out_shape=[jax.ShapeDtypeStruct((db, 1, n_pages * PAGE_SIZE), F32),
                   jax.ShapeDtypeStruct((db, 1, PAGE_SIZE), F32)],
        compiler_params=_params("arbitrary", "arbitrary"),
        name="idx_sample",
    )(pt_flat, iq3, iw3, new_page, *([cache_kidx] * per_step))


def _select_sample_kernel(sc_ref, scn_ref, bias_ref, biasn_ref, key_scr, *, topk):
    db = sc_ref.shape[0]
    key_new = _sort_key(scn_ref[:, 0:1])
    key_scr[...] = _sort_key(sc_ref[...])

    def count_ge(t):
        past = jnp.sum(jnp.where(key_scr[...] >= t, 1.0, 0.0), axis=1, keepdims=True)
        return past + jnp.where(key_new >= t, 1.0, 0.0)

    thr = _kth_largest_key(count_ge, (db, 1), topk)
    bias_ref[...] = jnp.where(key_scr[...] >= thr, 0.0, -jnp.inf)
    biasn_ref[...] = jnp.where(key_new >= thr, 0.0, -jnp.inf)


def _select_sample(scores, scores_new, topk):
    db, past = scores.shape
    return pl.pallas_call(
        functools.partial(_select_sample_kernel, topk=topk),
        out_shape=[jax.ShapeDtypeStruct((db, past), F32), jax.ShapeDtypeStruct((db, 1), F32)],
        scratch_shapes=[pltpu.VMEM((db, past), jnp.int32)],
        compiler_params=pltpu.CompilerParams(vmem_limit_bytes=V7X_VMEM_LIMIT_BYTES),
        name="select_sample",
    )(scores, scores_new)


def _attn_sample_kernel(pt_ref, q_ref, bias_ref, biasn_ref, kn_ref, vn_ref, *refs):
    del pt_ref
    per_step = (len(refs) - 4) // 2
    k_refs = refs[:per_step]
    v_refs = refs[per_step:2 * per_step]
    o_ref, m_scr, l_scr, acc_scr = refs[2 * per_step:]
    c = pl.program_id(1)
    scale = HEAD_DIM ** -0.5

    @pl.when(c == 0)
    def _():
        m_scr[...] = jnp.full(m_scr.shape, -jnp.inf, F32)
        l_scr[...] = jnp.zeros(l_scr.shape, F32)
        acc_scr[...] = jnp.zeros(acc_scr.shape, F32)

    def update(g, s, v):
        m_old = m_scr[g]
        m_new = jnp.maximum(m_old, jnp.max(s, axis=-1, keepdims=True))
        safe = jnp.where(m_new == -jnp.inf, 0.0, m_new)
        alpha = jnp.exp(m_old - safe)
        p = jnp.exp(s - safe)
        l_scr[g] = alpha * l_scr[g] + jnp.sum(p, axis=-1, keepdims=True)
        acc_scr[g] = alpha * acc_scr[g] + _dot(p.astype(BF16), v)
        m_scr[g] = m_new

    bias = bias_ref[0]
    for g in range(N_KV_HEADS):
        kg = jnp.concatenate([r[0, 0, pl.ds(g, PAGE_SIZE, stride=N_KV_HEADS), :] for r in k_refs], axis=0)
        vg = jnp.concatenate([r[0, 0, pl.ds(g, PAGE_SIZE, stride=N_KV_HEADS), :] for r in v_refs], axis=0)
        s = _dot_t(q_ref[0, g], kg.astype(BF16)) * scale + bias
        update(g, s, vg.astype(BF16))

    @pl.when(c == pl.num_programs(1) - 1)
    def _():
        for g in range(N_KV_HEADS):
            kn = kn_ref[0, :, g * HEAD_DIM:(g + 1) * HEAD_DIM].astype(BF16)
            vn = vn_ref[0, :, g * HEAD_DIM:(g + 1) * HEAD_DIM].astype(BF16)
            qf = q_ref[0, g].astype(F32)
            s = jnp.sum(qf * kn.astype(F32), axis=-1, keepdims=True) * scale + biasn_ref[0]
            m_old = m_scr[g]
            m_new = jnp.maximum(m_old, s)
            alpha = jnp.exp(m_old - m_new)
            p = jnp.exp(s - m_new)
            l = alpha * l_scr[g] + p
            acc = alpha * acc_scr[g] + p.astype(BF16).astype(F32) * vn.astype(F32)
            o_ref[0, g] = acc / l


def _attn_sample(pt_flat, q4, bias3, bias_new, k_new, v_new, cache_k, cache_v, layer, n_pages):
    db, _, rows, _ = q4.shape
    per_step = min(n_pages, ATTN_PAGES_PER_STEP)
    assert n_pages % per_step == 0
    page_block = (1, 1, PAGE_SIZE * N_KV_HEADS, HEAD_DIM)
    pages = _page_specs(layer, page_block, n_pages, per_step)
    grid_spec = pltpu.PrefetchScalarGridSpec(
        num_scalar_prefetch=1,
        grid=(db, n_pages // per_step),
        in_specs=[pl.BlockSpec((1, N_KV_HEADS, rows, HEAD_DIM), lambda b, c, pt: (b, 0, 0, 0)),
                  pl.BlockSpec((1, 1, per_step * PAGE_SIZE), lambda b, c, pt: (b, 0, c)),
                  pl.BlockSpec((1, 1, 1), lambda b, c, pt: (b, 0, 0)),
                  pl.BlockSpec((1, 1, KV_WIDTH), lambda b, c, pt: (b, 0, 0)),
                  pl.BlockSpec((1, 1, KV_WIDTH), lambda b, c, pt: (b, 0, 0))] + pages + pages,
        out_specs=pl.BlockSpec((1, N_KV_HEADS, rows, HEAD_DIM), lambda b, c, pt: (b, 0, 0, 0)),
        scratch_shapes=[pltpu.VMEM((N_KV_HEADS, rows, 1), F32),
                        pltpu.VMEM((N_KV_HEADS, rows, 1), F32),
                        pltpu.VMEM((N_KV_HEADS, rows, HEAD_DIM), F32)],
    )
    return pl.pallas_call(
        _attn_sample_kernel,
        grid_spec=grid_spec,
        out_shape=jax.ShapeDtypeStruct((db, N_KV_HEADS, rows, HEAD_DIM), F32),
        compiler_params=_params("arbitrary", "arbitrary"),
        name="attn_sample",
    )(pt_flat, q4, bias3, bias_new, k_new, v_new, *([cache_k] * per_step), *([cache_v] * per_step))


def _mix_sample_kernel(x_ref, a_ref, e_ref, wo_ref, pw_ref, ps_ref, g_ref, y_ref):
    group = pw_ref.shape[1]
    n_rows = e_ref.shape[0]
    d_parts = []
    for g, w in enumerate(POOL_WINDOWS):
        cols = slice(g * group, (g + 1) * group)
        ug = e_ref[n_rows - 1, :, cols]
        wsum = ug
        for s in range(1, w):
            wsum = wsum + e_ref[n_rows - 1 - s, :, cols]
        d_parts.append(wsum / float(w) - ug)
    p = _pool_project(d_parts, pw_ref, ps_ref)
    o = _dot(a_ref[...].astype(BF16), wo_ref[0:ATTN_WIDTH, :]) + _dot(p, wo_ref[ATTN_WIDTH:, :])
    y_ref[...] = x_ref[...] + _rms(o, g_ref[...])


def _mix_sample(x, a, e, wo, pw, ps, g):
    return pl.pallas_call(
        _mix_sample_kernel,
        out_shape=jax.ShapeDtypeStruct(x.shape, F32),
        compiler_params=pltpu.CompilerParams(vmem_limit_bytes=V7X_VMEM_LIMIT_BYTES),
        name="mix_sample",
    )(x, a, e, wo, pw, ps, g)


def _ffn_sample_kernel(x_ref, gpre_ref, wg_ref, wv_ref, cwg_ref, cwv_ref, cbg_ref, cbv_ref, sg_ref, sv_ref,
                       wd_ref, gpost_ref, y_ref, hg_ref, hv_ref, xn_scr, acc_scr):
    j = pl.program_id(0)

    @pl.when(j == 0)
    def _():
        xn_scr[...] = _rms(x_ref[...], gpre_ref[...]).astype(BF16)
        acc_scr[...] = jnp.zeros(acc_scr.shape, F32)

    xn = xn_scr[...]
    hg = _dot(xn, wg_ref[...])
    hv = _dot(xn, wv_ref[...])
    hg_ref[...] = hg
    hv_ref[...] = hv

    def conv(h, s_ref, cw_ref, cb_ref):
        return cb_ref[...] + cw_ref[0:1, :] * s_ref[0] + cw_ref[1:2, :] * s_ref[1] + cw_ref[2:3, :] * h

    cg = conv(hg, sg_ref, cwg_ref, cbg_ref)
    cv = conv(hv, sv_ref, cwv_ref, cbv_ref)
    act = (cg * jax.nn.sigmoid(cg) * cv).astype(BF16)
    acc_scr[...] += _dot(act, wd_ref[...])

    @pl.when(j == pl.num_programs(0) - 1)
    def _():
        y_ref[...] = x_ref[...] + _rms(acc_scr[...], gpost_ref[...])


def _ffn_sample(x, gpre, wup, cw, cb, state, wd, gpost, layer, tf):
    db, d = x.shape
    d_ff = wd.shape[1]
    nj = d_ff // tf
    n_state = state.shape[0]
    return pl.pallas_call(
        _ffn_sample_kernel,
        grid=(nj,),
        in_specs=[
            pl.BlockSpec((db, d), lambda j: (0, 0)),
            pl.BlockSpec((1, d), lambda j: (0, 0)),
            pl.BlockSpec((None, d, tf), lambda j: (layer, 0, j)),
            pl.BlockSpec((None, d, tf), lambda j: (layer, 0, j + nj)),
            pl.BlockSpec((CONV_WIDTH, tf), lambda j: (0, j)),
            pl.BlockSpec((CONV_WIDTH, tf), lambda j: (0, j + nj)),
            pl.BlockSpec((1, tf), lambda j: (0, j)),
            pl.BlockSpec((1, tf), lambda j: (0, j + nj)),
            pl.BlockSpec((n_state, db, tf), lambda j: (0, 0, j)),
            pl.BlockSpec((n_state, db, tf), lambda j: (0, 0, j + nj)),
            pl.BlockSpec((None, tf, d), lambda j: (layer, j, 0)),
            pl.BlockSpec((1, d), lambda j: (0, 0)),
        ],
        out_specs=[pl.BlockSpec((db, d), lambda j: (0, 0)),
                   pl.BlockSpec((db, tf), lambda j: (0, j)),
                   pl.BlockSpec((db, tf), lambda j: (0, j))],
        out_shape=[jax.ShapeDtypeStruct((db, d), F32), jax.ShapeDtypeStruct((db, d_ff), F32),
                   jax.ShapeDtypeStruct((db, d_ff), F32)],
        scratch_shapes=[pltpu.VMEM((db, d), BF16), pltpu.VMEM((db, d), F32)],
        compiler_params=_params("arbitrary"),
        name="ffn_sample",
    )(x, gpre, wup, wup, cw, cw, cb, cb, state, state, wd, gpost)


def _split_w_in(w_in_l):
    d = w_in_l.shape[0]
    q_end = ATTN_WIDTH
    k_end = q_end + KV_WIDTH
    v_end = k_end + KV_WIDTH
    iq_end = v_end + N_IDX_HEADS * IDX_DIM
    ik_end = iq_end + IDX_DIM
    iw_end = ik_end + N_IDX_HEADS
    w = w_in_l.astype(BF16)
    w_ik = w[:, iq_end:ik_end]
    zeros = lambda c: jnp.zeros((d, c), BF16)
    idx = jnp.concatenate([w_ik, zeros(V7X_LANES - IDX_DIM), zeros(V7X_LANES - IDX_DIM), w_ik,
                           w[:, ik_end:iw_end], zeros(V7X_LANES - N_IDX_HEADS)], axis=1)
    return {"q": w[:, :q_end], "kv": w[:, q_end:v_end], "iq": w[:, v_end:iq_end], "idx": idx, "u": w[:, iw_end:]}


def kernel(x_prompt, x_sample, cache_k, cache_v, cache_kidx, state_pool, state_conv, page_table, meta_tokens,
           w_in, w_o, pool_w, pool_scale, g_mix_pre, g_mix_post, g_ffn_pre, g_ffn_post, w_up, conv_w, conv_b,
           w_down):
    b, seq, d = x_prompt.shape
    db, ds, _ = x_sample.shape
    assert ds == 1, "the sample kernels handle one new token per sequence"
    depth = w_in.shape[0]
    n_pool = cache_k.shape[1]
    n_pages = page_table.shape[1]
    assert cache_k.shape[2] == PAGE_SIZE
    past = n_pages * PAGE_SIZE
    t_len = seq + N_META
    d_ff = w_down.shape[1]
    pool_width = d - ATTN_WIDTH
    topk_p = min(TOPK_MAX, t_len // 4)
    topk_s = min(TOPK_MAX, (past + ds) // 4)
    assert past >= max(POOL_WINDOWS)

    m_rows = b * t_len
    tm_proj = _row_tile(m_rows, min(TM_PROJ_CAP, t_len))
    tm_ffn = _row_tile(m_rows, min(TM_FFN_CAP, t_len))
    tf = _ff_tile(d_ff)
    sub8 = V7X_SUBLANES_F32
    heads_pad = -(-HEADS_PER_KV // sub8) * sub8

    meta = jnp.broadcast_to(meta_tokens[None].astype(x_prompt.dtype), (b, N_META, d))
    xp = jnp.concatenate([meta, x_prompt], axis=1).reshape(b * t_len, d)
    xs = x_sample.reshape(db, d)
    pt_flat = page_table.reshape(-1).astype(jnp.int32)
    ck = cache_k.reshape(depth, n_pool, PAGE_SIZE * N_KV_HEADS, HEAD_DIM)
    cv = cache_v.reshape(depth, n_pool, PAGE_SIZE * N_KV_HEADS, HEAD_DIM)
    kidx_t = jnp.swapaxes(cache_kidx, 2, 3)

    row2 = lambda a: a.reshape(1, -1)
    outs = {name: [] for name in ("kp", "vp", "ikp", "poolp", "convp", "ks", "vs", "iks", "pools", "convs")}
    wup = w_up.astype(BF16)
    wd = w_down.astype(BF16)
    for l in range(depth):
        w = _split_w_in(w_in[l])
        wo = w_o[l].astype(BF16)
        pw = pool_w[l].astype(BF16)
        ps = row2(pool_scale[l])
        cw = conv_w[l]
        cb = row2(conv_b[l])

        q, k, v, kb, vb, iq, ik, ik2, iw, u = _in_proj(xp, row2(g_mix_pre[l]), w, tm_proj)
        b3 = lambda a: a.reshape(b, t_len, a.shape[-1])
        a = _attn_prompt(b3(q), b3(iq), jnp.swapaxes(b3(iw), 1, 2), b3(kb), b3(vb), b3(ik2), topk_p)
        xp = _mix_prompt(xp, a.reshape(m_rows, ATTN_WIDTH), u, wo, pw, ps, row2(g_mix_post[l]), tm_proj, t_len)
        xp, last_g, last_v = _ffn_prompt(xp, row2(g_ffn_pre[l]), wup, cw, cb, wd, row2(g_ffn_post[l]), l,
                                         tm_ffn, tf, t_len)
        outs["kp"].append(k.reshape(b, t_len, N_KV_HEADS, HEAD_DIM))
        outs["vp"].append(v.reshape(b, t_len, N_KV_HEADS, HEAD_DIM))
        outs["ikp"].append(ik.reshape(b, t_len, IDX_DIM))
        outs["poolp"].append(u.reshape(b, t_len, pool_width)[:, t_len - POOL_BUF:])
        tail = slice(CONV_HALO - (CONV_WIDTH - 1), CONV_HALO)
        if t_len % tm_ffn == 0:
            per_seq = t_len // tm_ffn
            seq_last = lambda h: h[per_seq - 1::per_seq, tail]
        else:
            end_tiles = [((s + 1) * t_len - 1) // tm_ffn for s in range(b)]
            seq_last = lambda h: jnp.stack([h[i, tail] for i in end_tiles])
        outs["convp"].append(jnp.concatenate([seq_last(last_g), seq_last(last_v)], axis=-1))

        q, k, v, _, _, iq, ik, _, iw, u = _in_proj(xs, row2(g_mix_pre[l]), w, db)
        iq3 = iq.reshape(db, N_IDX_HEADS, IDX_DIM)
        new_page = jnp.pad(ik[:, :, None], ((0, 0), (0, 0), (0, PAGE_SIZE - 1)))
        scores, scores_new = _idx_sample(pt_flat, iq3, iw.reshape(db, N_IDX_HEADS, 1), new_page, kidx_t, l, n_pages)
        bias, bias_new = _select_sample(scores.reshape(db, past), scores_new.reshape(db, PAGE_SIZE), topk_s)
        q4 = jnp.pad(q.reshape(db, N_KV_HEADS, HEADS_PER_KV, HEAD_DIM),
                     ((0, 0), (0, 0), (0, heads_pad - HEADS_PER_KV), (0, 0)))
        a = _attn_sample(pt_flat, q4, bias.reshape(db, 1, past), bias_new.reshape(db, 1, 1),
                         k.reshape(db, 1, KV_WIDTH), v.reshape(db, 1, KV_WIDTH), ck, cv, l, n_pages)
        a = a[:, :, :HEADS_PER_KV].reshape(db, ATTN_WIDTH)
        e = jnp.concatenate([jnp.swapaxes(state_pool[l], 0, 1).astype(u.dtype), u[None]], axis=0)
        xs = _mix_sample(xs, a, e, wo, pw, ps, row2(g_mix_post[l]))
        xs, hg, hv = _ffn_sample(xs, row2(g_ffn_pre[l]), wup, cw, cb, jnp.swapaxes(state_conv[l], 0, 1), wd,
                                 row2(g_ffn_post[l]), l, tf)
        outs["ks"].append(k.reshape(db, ds, N_KV_HEADS, HEAD_DIM))
        outs["vs"].append(v.reshape(db, ds, N_KV_HEADS, HEAD_DIM))
        outs["iks"].append(ik.reshape(db, ds, IDX_DIM))
        outs["pools"].append(jnp.concatenate([state_pool[l].astype(u.dtype), u[:, None, :]], axis=1)[:, 1:])
        h = jnp.concatenate([hg, hv], axis=-1)[:, None, :]
        outs["convs"].append(jnp.concatenate([state_conv[l].astype(h.dtype), h], axis=1)[:, -(CONV_WIDTH - 1):])

    y_prompt = xp.reshape(b, t_len, d)[:, N_META:]
    y_sample = xs.reshape(db, ds, d)
    stack = lambda name: jnp.stack(outs[name])
    return (y_prompt, y_sample, stack("kp"), stack("vp"), stack("ikp"), stack("poolp"), stack("convp"),
            stack("ks"), stack("vs"), stack("iks"), stack("pools"), stack("convs"))
```

```python
import functools

import jax
import jax.numpy as jnp
from jax import lax
from jax.experimental import pallas as pl
from jax.experimental.pallas import tpu as pltpu

N_META = 16
N_HEADS = 8
N_KV_HEADS = 2
HEAD_DIM = 128
N_IDX_HEADS = 16
IDX_DIM = 64
TOPK_MAX = 256
POOL_WINDOWS = (2, 4, 8, 16)
CONV_WIDTH = 3
PAGE_SIZE = 128
Q_BLOCK = 256
RMS_EPS = 1e-6

ATTN_WIDTH = N_HEADS * HEAD_DIM
KV_WIDTH = N_KV_HEADS * HEAD_DIM
HEADS_PER_KV = N_HEADS // N_KV_HEADS
POOL_BUF = max(POOL_WINDOWS) - 1

V7X_LANES = 128
V7X_SUBLANES_F32 = 8
V7X_SUBLANES_BF16 = 16
V7X_VMEM_LIMIT_BYTES = 56 * 1024 * 1024

INT_MIN = -(2 ** 31)
F32 = jnp.float32
BF16 = jnp.bfloat16

CONV_HALO = V7X_SUBLANES_F32
POOL_HALO = 2 * V7X_SUBLANES_F32
IDX_PAGES_PER_STEP = 64
ATTN_PAGES_PER_STEP = 32
KEY_CHUNK = 256
TM_PROJ_CAP = 384
TM_FFN_CAP = 688


def _params(*sem):
    return pltpu.CompilerParams(dimension_semantics=sem, vmem_limit_bytes=V7X_VMEM_LIMIT_BYTES)


def _row_tile(seq_len, cap):
    best = None
    for d in range(V7X_SUBLANES_BF16, min(seq_len, cap) + 1, V7X_SUBLANES_BF16):
        if seq_len % d == 0:
            best = d
    assert best is not None, seq_len
    return best


def _ff_tile(d_ff, cap=512):
    best = None
    for d in range(V7X_LANES, min(d_ff, cap) + 1, V7X_LANES):
        if d_ff % d == 0:
            best = d
    assert best is not None, d_ff
    return best


def _rms(x, g):
    return x * lax.rsqrt(jnp.mean(x * x, axis=-1, keepdims=True) + RMS_EPS) * g


def _dot(a, b):
    return jnp.dot(a, b, preferred_element_type=F32)


def _dot_t(a, b):
    return lax.dot_general(a, b, (((1,), (1,)), ((), ())), preferred_element_type=F32)


def _const_spec(shape):
    n = len(shape)
    return pl.BlockSpec(shape, lambda *_: (0,) * n)


def _in_proj_kernel(x_ref, g_ref, wq_ref, wkv_ref, wiq_ref, widx_ref, wu_ref,
                    q_ref, k_ref, v_ref, kb_ref, vb_ref, iq_ref, ik_ref, ik2_ref, iw_ref, u_ref):
    hn = _rms(x_ref[...], g_ref[...]).astype(BF16)
    q_ref[...] = _dot(hn, wq_ref[...]).astype(BF16)
    kv = _dot(hn, wkv_ref[...])
    tm = x_ref.shape[0]
    for out_ref, c0 in ((k_ref, 0), (v_ref, KV_WIDTH)):
        for g in range(N_KV_HEADS):
            out_ref[pl.ds(g, tm, stride=N_KV_HEADS), :] = kv[:, c0 + g * HEAD_DIM:c0 + (g + 1) * HEAD_DIM]
    kb_ref[...] = kv[:, :KV_WIDTH].astype(BF16)
    vb_ref[...] = kv[:, KV_WIDTH:].astype(BF16)
    iq_ref[...] = _dot(hn, wiq_ref[...]).astype(BF16)
    idx = _dot(hn, widx_ref[...])
    ik_ref[...] = idx[:, :IDX_DIM]
    ik2_ref[...] = idx[:, :2 * V7X_LANES].astype(BF16)
    iw_ref[...] = idx[:, 2 * V7X_LANES:2 * V7X_LANES + N_IDX_HEADS]
    u_ref[...] = _dot(hn, wu_ref[...])


def _in_proj(x, g, w, tm):
    m, d = x.shape
    pool_width = w["u"].shape[1]
    row = lambda c: pl.BlockSpec((tm, c), lambda i: (i, 0))
    outs = [
        (1, ATTN_WIDTH, BF16), (N_KV_HEADS, HEAD_DIM, F32), (N_KV_HEADS, HEAD_DIM, F32), (1, KV_WIDTH, BF16),
        (1, KV_WIDTH, BF16), (1, N_IDX_HEADS * IDX_DIM, BF16), (1, IDX_DIM, F32), (1, 2 * V7X_LANES, BF16),
        (1, N_IDX_HEADS, F32), (1, pool_width, F32),
    ]
    return pl.pallas_call(
        _in_proj_kernel,
        grid=(m // tm,),
        in_specs=[row(d), _const_spec((1, d)), _const_spec(w["q"].shape), _const_spec(w["kv"].shape),
                  _const_spec(w["iq"].shape), _const_spec(w["idx"].shape), _const_spec(w["u"].shape)],
        out_specs=[pl.BlockSpec((r * tm, c), lambda i: (i, 0)) for r, c, _ in outs],
        out_shape=[jax.ShapeDtypeStruct((r * m, c), dt) for r, c, dt in outs],
        compiler_params=_params("arbitrary"),
        name="in_proj",
    )(x, g, w["q"], w["kv"], w["iq"], w["idx"], w["u"])


def _sort_key(x):
    bits = lax.bitcast_convert_type(x, jnp.int32)
    return bits ^ (lax.shift_right_arithmetic(bits, 31) & jnp.int32(0x7FFFFFFF))


def _kth_largest_key(count_ge, shape, k):
    def body(it, state):
        t, cnt_t = state
        cand = t | lax.shift_left(jnp.int32(1), 31 - it)
        cnt = count_ge(cand ^ jnp.int32(INT_MIN))
        take = cnt >= k
        return jnp.where(take, cand, t), jnp.where(take, cnt, cnt_t)

    t, cnt_t = lax.fori_loop(0, 32, body, (jnp.zeros(shape, jnp.int32), jnp.zeros(shape, F32)))
    return t ^ jnp.int32(INT_MIN), cnt_t


def _tie_rank_matrix(n, lower):
    r = lax.broadcasted_iota(jnp.int32, (n, n), 0)
    c = lax.broadcasted_iota(jnp.int32, (n, n), 1)
    return jnp.where(r >= c if lower else r <= c, 1.0, 0.0).astype(BF16)


def _attn_prompt_kernel(q_ref, iq_ref, iwt_ref, kb_ref, vb_ref, ik2_ref, o_ref,
                        k_scr, v_scr, ik_scr, key_scr, thr_scr, cnt_scr, bias_scr, s_scr, m_scr, l_scr, acc_scr,
                        *, topk):
    i = pl.program_id(1)
    t_len = kb_ref.shape[1]
    n_chunks, ch, _ = key_scr.shape
    tk = n_chunks * ch
    halves = ch // V7X_LANES
    rows = HEADS_PER_KV * Q_BLOCK
    nc = ((i + 1) * Q_BLOCK + ch - 1) // ch

    @pl.when(i == 0)
    def _():
        k_scr[0:t_len, :] = kb_ref[0]
        v_scr[0:t_len, :] = vb_ref[0]
        ik_scr[0, 0:t_len, :] = ik2_ref[0, :, 0:V7X_LANES]
        ik_scr[1, 0:t_len, :] = ik2_ref[0, :, V7X_LANES:]
        if tk > t_len:
            k_scr[t_len:tk, :] = jnp.zeros((tk - t_len, k_scr.shape[1]), BF16)
            v_scr[t_len:tk, :] = jnp.zeros((tk - t_len, v_scr.shape[1]), BF16)
            ik_scr[0, t_len:tk, :] = jnp.zeros((tk - t_len, V7X_LANES), BF16)
            ik_scr[1, t_len:tk, :] = jnp.zeros((tk - t_len, V7X_LANES), BF16)

    iw = iwt_ref[0] * (IDX_DIM ** -0.5 * N_IDX_HEADS ** -0.5)
    iq2 = [jnp.concatenate([iq_ref[0, :, (2 * jj + e) * V7X_LANES:(2 * jj + e + 1) * V7X_LANES] for e in range(2)],
                           axis=0) for jj in range(N_IDX_HEADS // 4)]
    qpos = i * Q_BLOCK + lax.broadcasted_iota(jnp.int32, (1, Q_BLOCK), 1)

    def over_chunks(chunk_fn):
        def pair(p, carry):
            chunk_fn(2 * p)
            chunk_fn(2 * p + 1)
            return carry

        lax.fori_loop(0, lax.shift_right_logical(nc, 1), pair, 0)

        @pl.when((nc & 1) == 1)
        def _():
            chunk_fn(nc - 1)

    def index_chunk(c):
        r0 = pl.multiple_of(c * ch, ch)
        keys = jnp.concatenate([ik_scr[0, pl.ds(r0, ch), :], ik_scr[1, pl.ds(r0, ch), :]], axis=0)
        acc = jnp.zeros((ch, Q_BLOCK), F32)
        for jj in range(N_IDX_HEADS // 4):
            s = _dot_t(keys, iq2[jj])
            for e in range(2):
                for half in range(2):
                    h = 2 * (2 * jj + e) + half
                    blk = s[half * ch:(half + 1) * ch, e * Q_BLOCK:(e + 1) * Q_BLOCK]
                    acc = acc + jnp.maximum(blk, 0.0) * iw[h:h + 1, :]
        kpos = r0 + lax.broadcasted_iota(jnp.int32, (ch, 1), 0)
        key_scr[c] = jnp.where(kpos <= qpos, _sort_key(acc), jnp.int32(INT_MIN))

    over_chunks(index_chunk)

    for n_static in range(1, n_chunks + 1):
        @pl.when(nc == n_static)
        def _(n_static=n_static):
            def count_ge(t):
                part = jnp.zeros((V7X_SUBLANES_F32, Q_BLOCK), F32)
                for c in range(n_static):
                    hit = jnp.where(key_scr[c] >= t, 1.0, 0.0)
                    part = part + jnp.sum(hit.reshape(ch // V7X_SUBLANES_F32, V7X_SUBLANES_F32, Q_BLOCK), axis=0)
                return jnp.sum(part, axis=0, keepdims=True)

            t, cnt_t = _kth_largest_key(count_ge, (1, Q_BLOCK), topk)
            thr_scr[...] = jnp.broadcast_to(jnp.maximum(t, jnp.int32(INT_MIN + 1)), thr_scr.shape)
            cnt_scr[...] = jnp.broadcast_to(cnt_t, cnt_scr.shape)

    thr = thr_scr[0:1, :]
    tied = jnp.max(jnp.where(cnt_scr[0:1, :] > topk, 1.0, 0.0)) > 0.0

    def store_bias(c, bias_t):
        for r in range(halves):
            bias_scr[c, :, r * V7X_LANES:(r + 1) * V7X_LANES] = bias_t[r * V7X_LANES:(r + 1) * V7X_LANES, :].T

    @pl.when(jnp.logical_not(tied))
    def _():
        over_chunks(lambda c: store_bias(c, jnp.where(key_scr[c] >= thr, 0.0, -jnp.inf)))

    @pl.when(tied)
    def _():
        def count_above(c, acc):
            return acc + jnp.sum(jnp.where(key_scr[c] > thr, 1.0, 0.0), axis=0, keepdims=True)

        room = topk - lax.fori_loop(0, nc, count_above, jnp.zeros((1, Q_BLOCK), F32))
        rank = _tie_rank_matrix(ch, lower=True)

        def tie_chunk(c, seen):
            key = key_scr[c]
            eq = jnp.where(key == thr, 1.0, 0.0)
            pos = _dot(rank, eq.astype(BF16)) + seen
            keep = jnp.where(key > thr, 1.0, jnp.where(pos <= room, eq, 0.0))
            store_bias(c, jnp.where(keep > 0.0, 0.0, -jnp.inf))
            return seen + jnp.sum(eq, axis=0, keepdims=True)

        lax.fori_loop(0, nc, tie_chunk, jnp.zeros((1, Q_BLOCK), F32))

    scale = HEAD_DIM ** -0.5
    gcols = [slice(g * HEAD_DIM, (g + 1) * HEAD_DIM) for g in range(N_KV_HEADS)]
    qgs = [jnp.concatenate([q_ref[0, :, (g * HEADS_PER_KV + r) * HEAD_DIM:(g * HEADS_PER_KV + r + 1) * HEAD_DIM]
                            for r in range(HEADS_PER_KV)], axis=0) for g in range(N_KV_HEADS)]
    lane_tiles = lambda x: [x[:, r * V7X_LANES:(r + 1) * V7X_LANES] for r in range(halves)]

    m_scr[...] = jnp.full(m_scr.shape, -jnp.inf, F32)

    def score_chunk(c):
        r0 = pl.multiple_of(c * ch, ch)
        bias = bias_scr[c][None]
        for g in range(N_KV_HEADS):
            s = _dot_t(qgs[g], k_scr[pl.ds(r0, ch), gcols[g]])
            s = (s.reshape(HEADS_PER_KV, Q_BLOCK, ch) * scale + bias).reshape(rows, ch)
            s_scr[g, c] = s
            m = m_scr[g]
            for tile in lane_tiles(s):
                m = jnp.maximum(m, tile)
            m_scr[g] = m

    over_chunks(score_chunk)
    for g in range(N_KV_HEADS):
        m_scr[g] = jnp.broadcast_to(jnp.max(m_scr[g], axis=1, keepdims=True), m_scr.shape[1:])
    l_scr[...] = jnp.zeros(l_scr.shape, F32)
    acc_scr[...] = jnp.zeros(acc_scr.shape, F32)

    def value_chunk(c):
        r0 = pl.multiple_of(c * ch, ch)
        for g in range(N_KV_HEADS):
            m = m_scr[g]
            p_tiles = [jnp.exp(tile - m) for tile in lane_tiles(s_scr[g, c])]
            l = l_scr[g]
            for tile in p_tiles:
                l = l + tile
            l_scr[g] = l
            p = jnp.concatenate(p_tiles, axis=1).astype(BF16)
            acc_scr[g] += _dot(p, v_scr[pl.ds(r0, ch), gcols[g]])

    over_chunks(value_chunk)
    for g in range(N_KV_HEADS):
        o = acc_scr[g] / jnp.sum(l_scr[g], axis=1, keepdims=True)
        for r in range(HEADS_PER_KV):
            h = g * HEADS_PER_KV + r
            o_ref[0, :, h * HEAD_DIM:(h + 1) * HEAD_DIM] = o[r * Q_BLOCK:(r + 1) * Q_BLOCK].astype(o_ref.dtype)


def _attn_prompt(q, iq, iwt, kb, vb, ik2, topk):
    b, t_len, _ = q.shape
    n_blk = pl.cdiv(t_len, Q_BLOCK)
    n_chunks = pl.cdiv(n_blk * Q_BLOCK, KEY_CHUNK)
    tk = n_chunks * KEY_CHUNK
    rows = HEADS_PER_KV * Q_BLOCK
    qblk = lambda c: pl.BlockSpec((1, Q_BLOCK, c), lambda bi, i: (bi, i, 0))
    full = lambda c: pl.BlockSpec((1, t_len, c), lambda bi, i: (bi, 0, 0))
    return pl.pallas_call(
        functools.partial(_attn_prompt_kernel, topk=topk),
        grid=(b, n_blk),
        in_specs=[qblk(ATTN_WIDTH), qblk(N_IDX_HEADS * IDX_DIM),
                  pl.BlockSpec((1, N_IDX_HEADS, Q_BLOCK), lambda bi, i: (bi, 0, i)),
                  full(KV_WIDTH), full(KV_WIDTH), full(2 * V7X_LANES)],
        out_specs=qblk(ATTN_WIDTH),
        out_shape=jax.ShapeDtypeStruct((b, t_len, ATTN_WIDTH), BF16),
        scratch_shapes=[pltpu.VMEM((tk, KV_WIDTH), BF16), pltpu.VMEM((tk, KV_WIDTH), BF16),
                        pltpu.VMEM((2, tk, V7X_LANES), BF16),
                        pltpu.VMEM((n_chunks, KEY_CHUNK, Q_BLOCK), jnp.int32),
                        pltpu.VMEM((V7X_SUBLANES_F32, Q_BLOCK), jnp.int32),
                        pltpu.VMEM((V7X_SUBLANES_F32, Q_BLOCK), F32),
                        pltpu.VMEM((n_chunks, Q_BLOCK, KEY_CHUNK), F32),
                        pltpu.VMEM((N_KV_HEADS, n_chunks, rows, KEY_CHUNK), F32),
                        pltpu.VMEM((N_KV_HEADS, rows, V7X_LANES), F32),
                        pltpu.VMEM((N_KV_HEADS, rows, V7X_LANES), F32),
                        pltpu.VMEM((N_KV_HEADS, rows, HEAD_DIM), F32)],
        compiler_params=_params("arbitrary", "arbitrary"),
        name="attn_prompt",
    )(q, iq, iwt, kb, vb, ik2)


def _pool_project(d_parts, pw_ref, ps_ref):
    group = d_parts[0].shape[1]
    outs = []
    for g, d in enumerate(d_parts):
        outs.append(_dot(d.astype(BF16), pw_ref[g]) * ps_ref[:, g * group:(g + 1) * group])
    return jnp.concatenate(outs, axis=1).astype(BF16)


def _pos_in_seq(i, tm, seq_len):
    t = (i * tm) % seq_len + lax.broadcasted_iota(jnp.int32, (tm, 1), 0)
    return jnp.where(t >= seq_len, t - seq_len, t)


def _mix_prompt_kernel(x_ref, a_ref, u_ref, wo_ref, pw_ref, ps_ref, g_ref, y_ref, e_scr, *, seq_len):
    i = pl.program_id(0)
    tm = x_ref.shape[0]
    group = pw_ref.shape[1]

    @pl.when(i == 0)
    def _():
        e_scr[0:POOL_HALO, :] = jnp.zeros((POOL_HALO, e_scr.shape[1]), F32)

    u = u_ref[...]
    e_scr[POOL_HALO:POOL_HALO + tm, :] = u
    t = _pos_in_seq(i, tm, seq_len)
    in_seq = [None] + [(t >= s).astype(F32) for s in range(1, max(POOL_WINDOWS))]
    d_parts = []
    for g, w in enumerate(POOL_WINDOWS):
        cols = slice(g * group, (g + 1) * group)
        ug = u[:, cols]
        wsum = ug
        for s in range(1, w):
            wsum = wsum + in_seq[s] * e_scr[POOL_HALO - s:POOL_HALO - s + tm, cols]
        cnt = jnp.minimum(t + 1, w).astype(F32)
        d_parts.append(wsum / cnt - ug)
    e_scr[0:POOL_HALO, :] = u[tm - POOL_HALO:, :]
    p = _pool_project(d_parts, pw_ref, ps_ref)
    o = _dot(a_ref[...], wo_ref[0:ATTN_WIDTH, :]) + _dot(p, wo_ref[ATTN_WIDTH:, :])
    y_ref[...] = x_ref[...] + _rms(o, g_ref[...])


def _mix_prompt(x, a, u, wo, pw, ps, g, tm, seq_len):
    m, d = x.shape
    pool_width = u.shape[1]
    assert POOL_HALO <= tm <= seq_len
    row = lambda c: pl.BlockSpec((tm, c), lambda i: (i, 0))
    return pl.pallas_call(
        functools.partial(_mix_prompt_kernel, seq_len=seq_len),
        grid=(m // tm,),
        in_specs=[row(d), row(ATTN_WIDTH), row(pool_width), _const_spec(wo.shape), _const_spec(pw.shape),
                  _const_spec((1, pool_width)), _const_spec((1, d))],
        out_specs=row(d),
        out_shape=jax.ShapeDtypeStruct((m, d), F32),
        scratch_shapes=[pltpu.VMEM((POOL_HALO + tm, pool_width), F32)],
        compiler_params=_params("arbitrary"),
        name="mix_prompt",
    )(x, a, u, wo, pw, ps, g)


def _ffn_prompt_kernel(x_ref, gpre_ref, wg_ref, wv_ref, cwg_ref, cwv_ref, cbg_ref, cbv_ref, wd_ref, gpost_ref,
                       y_ref, lastg_ref, lastv_ref,
                       xn_scr, acc_scr, hg_scr, hv_scr, carryg_scr, carryv_scr, *, seq_len):
    i = pl.program_id(0)
    j = pl.program_id(1)
    tm = x_ref.shape[0]

    @pl.when(j == 0)
    def _():
        xn_scr[...] = _rms(x_ref[...], gpre_ref[...]).astype(BF16)
        acc_scr[...] = jnp.zeros(acc_scr.shape, F32)

    @pl.when(i == 0)
    def _():
        carryg_scr[j] = jnp.zeros(carryg_scr.shape[1:], F32)
        carryv_scr[j] = jnp.zeros(carryv_scr.shape[1:], F32)

    t = _pos_in_seq(i, tm, seq_len)
    in_seq1 = (t >= 1).astype(F32)
    in_seq2 = (t >= 2).astype(F32)
    seq_end = jnp.minimum(seq_len - (i * tm) % seq_len, tm)
    last_row0 = pl.multiple_of(seq_end, CONV_HALO)

    xn = xn_scr[...]
    acts = []
    for w_ref, cw_ref, cb_ref, h_scr, carry_scr, last_ref in (
            (wg_ref, cwg_ref, cbg_ref, hg_scr, carryg_scr, lastg_ref),
            (wv_ref, cwv_ref, cbv_ref, hv_scr, carryv_scr, lastv_ref)):
        h = _dot(xn, w_ref[...])
        h_scr[0:CONV_HALO, :] = carry_scr[j]
        h_scr[CONV_HALO:CONV_HALO + tm, :] = h
        carry_scr[j] = h[tm - CONV_HALO:, :]
        last_ref[0] = h_scr[pl.ds(last_row0, CONV_HALO), :]
        acts.append(cb_ref[...]
                    + cw_ref[0:1, :] * (in_seq2 * h_scr[CONV_HALO - 2:CONV_HALO - 2 + tm, :])
                    + cw_ref[1:2, :] * (in_seq1 * h_scr[CONV_HALO - 1:CONV_HALO - 1 + tm, :])
                    + cw_ref[2:3, :] * h)
    cg, cv = acts
    act = (cg * jax.nn.sigmoid(cg) * cv).astype(BF16)
    acc_scr[...] += _dot(act, wd_ref[...])

    @pl.when(j == pl.num_programs(1) - 1)
    def _():
        y_ref[...] = x_ref[...] + _rms(acc_scr[...], gpost_ref[...])


def _ffn_prompt(x, gpre, wup, cw, cb, wd, gpost, layer, tm, tf, seq_len):
    m, d = x.shape
    d_ff = wd.shape[1]
    nj = d_ff // tf
    n_tiles = m // tm
    assert CONV_HALO <= tm <= seq_len and seq_len % CONV_HALO == 0 and tm % CONV_HALO == 0
    return pl.pallas_call(
        functools.partial(_ffn_prompt_kernel, seq_len=seq_len),
        grid=(n_tiles, nj),
        in_specs=[
            pl.BlockSpec((tm, d), lambda i, j: (i, 0)),
            pl.BlockSpec((1, d), lambda i, j: (0, 0)),
            pl.BlockSpec((None, d, tf), lambda i, j: (layer, 0, j)),
            pl.BlockSpec((None, d, tf), lambda i, j: (layer, 0, j + nj)),
            pl.BlockSpec((CONV_WIDTH, tf), lambda i, j: (0, j)),
            pl.BlockSpec((CONV_WIDTH, tf), lambda i, j: (0, j + nj)),
            pl.BlockSpec((1, tf), lambda i, j: (0, j)),
            pl.BlockSpec((1, tf), lambda i, j: (0, j + nj)),
            pl.BlockSpec((None, tf, d), lambda i, j: (layer, j, 0)),
            pl.BlockSpec((1, d), lambda i, j: (0, 0)),
        ],
        out_specs=[
            pl.BlockSpec((tm, d), lambda i, j: (i, 0)),
            pl.BlockSpec((1, CONV_HALO, tf), lambda i, j: (i, 0, j)),
            pl.BlockSpec((1, CONV_HALO, tf), lambda i, j: (i, 0, j)),
        ],
        out_shape=[jax.ShapeDtypeStruct((m, d), F32),
                   jax.ShapeDtypeStruct((n_tiles, CONV_HALO, d_ff), F32),
                   jax.ShapeDtypeStruct((n_tiles, CONV_HALO, d_ff), F32)],
        scratch_shapes=[pltpu.VMEM((tm, d), BF16), pltpu.VMEM((tm, d), F32),
                        pltpu.VMEM((CONV_HALO + tm, tf), F32), pltpu.VMEM((CONV_HALO + tm, tf), F32),
                        pltpu.VMEM((nj, CONV_HALO, tf), F32), pltpu.VMEM((nj, CONV_HALO, tf), F32)],
        compiler_params=_params("arbitrary", "arbitrary"),
        name="ffn_prompt",
    )(x, gpre, wup, wup, cw, cw, cb, cb, wd, gpost)


def _idx_sample_kernel(pt_ref, iq_ref, iw_ref, new_ref, *refs):
    del pt_ref
    page_refs, o_ref, onew_ref = refs[:-2], refs[-2], refs[-1]
    iq = iq_ref[0]
    iw = iw_ref[0] * (IDX_DIM ** -0.5 * N_IDX_HEADS ** -0.5)

    def scores(keys_t):
        s = _dot(iq, keys_t.astype(BF16))
        return jnp.sum(jnp.maximum(s, 0.0) * iw, axis=0, keepdims=True)

    o_ref[0] = scores(jnp.concatenate([r[0, 0] for r in page_refs], axis=1))
    onew_ref[0] = scores(new_ref[0])


def _page_specs(layer, block, n_pages, per_step):
    def spec(p):
        def index(b, c, pt):
            return (layer, pt[b * n_pages + c * per_step + p]) + (0,) * (len(block) - 2)
        return pl.BlockSpec(block, index)
    return [spec(p) for p in range(per_step)]


def _idx_sample(pt_flat, iq3, iw3, new_page, cache_kidx, layer, n_pages):
    db = iq3.shape[0]
    per_step = min(n_pages, IDX_PAGES_PER_STEP)
    assert n_pages % per_step == 0
    grid_spec = pltpu.PrefetchScalarGridSpec(
        num_scalar_prefetch=1,
        grid=(db, n_pages // per_step),
        in_specs=[pl.BlockSpec((1, N_IDX_HEADS, IDX_DIM), lambda b, c, pt: (b, 0, 0)),
                  pl.BlockSpec((1, N_IDX_HEADS, 1), lambda b, c, pt: (b, 0, 0)),
                  pl.BlockSpec((1, IDX_DIM, PAGE_SIZE), lambda b, c, pt: (b, 0, 0))]
                 + _page_specs(layer, (1, 1, IDX_DIM, PAGE_SIZE), n_pages, per_step),
        out_specs=[pl.BlockSpec((1, 1, per_step * PAGE_SIZE), lambda b, c, pt: (b, 0, c)),
                   pl.BlockSpec((1, 1, PAGE_SIZE), lambda b, c, pt: (b, 0, 0))],
    )
    return pl.pallas_call(
        _idx_sample_kernel,
        grid_spec=grid_spec,
        out_shape=[jax.ShapeDtypeStruct((db, 1, n_pages * PAGE_SIZE), F32),
                   jax.ShapeDtypeStruct((db, 1, PAGE_SIZE), F32)],
        compiler_params=_params("arbitrary", "arbitrary"),
        name="idx_sample",
    )(pt_flat, iq3, iw3, new_page, *([cache_kidx] * per_step))


def _select_sample_kernel(sc_ref, scn_ref, bias_ref, biasn_ref, key_scr, *, topk):
    db = sc_ref.shape[0]
    key_new = _sort_key(scn_ref[:, 0:1])
    key_scr[...] = _sort_key(sc_ref[...])

    def count_ge(t):
        past = jnp.sum(jnp.where(key_scr[...] >= t, 1.0, 0.0), axis=1, keepdims=True)
        return past + jnp.where(key_new >= t, 1.0, 0.0)

    thr, cnt_t = _kth_largest_key(count_ge, (db, 1), topk)
    tied = jnp.max(jnp.where(cnt_t > topk, 1.0, 0.0)) > 0.0

    @pl.when(jnp.logical_not(tied))
    def _():
        bias_ref[...] = jnp.where(key_scr[...] >= thr, 0.0, -jnp.inf)
        biasn_ref[...] = jnp.where(key_new >= thr, 0.0, -jnp.inf)

    @pl.when(tied)
    def _():
        above = (jnp.sum(jnp.where(key_scr[...] > thr, 1.0, 0.0), axis=1, keepdims=True)
                 + jnp.where(key_new > thr, 1.0, 0.0))
        room = topk - above
        rank = _tie_rank_matrix(V7X_LANES, lower=False)
        seen = jnp.zeros((db, 1), F32)
        for blk in range(sc_ref.shape[1] // V7X_LANES):
            cols = slice(blk * V7X_LANES, (blk + 1) * V7X_LANES)
            key = key_scr[:, cols]
            eq = jnp.where(key == thr, 1.0, 0.0)
            pos = _dot(eq.astype(BF16), rank) + seen
            keep = jnp.where(key > thr, 1.0, jnp.where(pos <= room, eq, 0.0))
            bias_ref[:, cols] = jnp.where(keep > 0.0, 0.0, -jnp.inf)
            seen = seen + jnp.sum(eq, axis=1, keepdims=True)
        keep_new = (key_new > thr) | ((key_new == thr) & (seen + 1.0 <= room))
        biasn_ref[...] = jnp.where(keep_new, 0.0, -jnp.inf)


def _select_sample(scores, scores_new, topk):
    db, past = scores.shape
    return pl.pallas_call(
        functools.partial(_select_sample_kernel, topk=topk),
        out_shape=[jax.ShapeDtypeStruct((db, past), F32), jax.ShapeDtypeStruct((db, 1), F32)],
        scratch_shapes=[pltpu.VMEM((db, past), jnp.int32)],
        compiler_params=pltpu.CompilerParams(vmem_limit_bytes=V7X_VMEM_LIMIT_BYTES),
        name="select_sample",
    )(scores, scores_new)


def _attn_sample_kernel(pt_ref, q_ref, bias_ref, biasn_ref, kn_ref, vn_ref, *refs):
    del pt_ref
    per_step = (len(refs) - 4) // 2
    k_refs = refs[:per_step]
    v_refs = refs[per_step:2 * per_step]
    o_ref, m_scr, l_scr, acc_scr = refs[2 * per_step:]
    c = pl.program_id(1)
    scale = HEAD_DIM ** -0.5

    @pl.when(c == 0)
    def _():
        m_scr[...] = jnp.full(m_scr.shape, -jnp.inf, F32)
        l_scr[...] = jnp.zeros(l_scr.shape, F32)
        acc_scr[...] = jnp.zeros(acc_scr.shape, F32)

    def update(g, s, v):
        m_old = m_scr[g]
        m_new = jnp.maximum(m_old, jnp.max(s, axis=-1, keepdims=True))
        safe = jnp.where(m_new == -jnp.inf, 0.0, m_new)
        alpha = jnp.exp(m_old - safe)
        p = jnp.exp(s - safe)
        l_scr[g] = alpha * l_scr[g] + jnp.sum(p, axis=-1, keepdims=True)
        acc_scr[g] = alpha * acc_scr[g] + _dot(p.astype(BF16), v)
        m_scr[g] = m_new

    bias = bias_ref[0]
    for g in range(N_KV_HEADS):
        kg = jnp.concatenate([r[0, 0, pl.ds(g, PAGE_SIZE, stride=N_KV_HEADS), :] for r in k_refs], axis=0)
        vg = jnp.concatenate([r[0, 0, pl.ds(g, PAGE_SIZE, stride=N_KV_HEADS), :] for r in v_refs], axis=0)
        s = _dot_t(q_ref[0, g], kg.astype(BF16)) * scale + bias
        update(g, s, vg.astype(BF16))

    @pl.when(c == pl.num_programs(1) - 1)
    def _():
        for g in range(N_KV_HEADS):
            kn = kn_ref[0, :, g * HEAD_DIM:(g + 1) * HEAD_DIM].astype(BF16)
            vn = vn_ref[0, :, g * HEAD_DIM:(g + 1) * HEAD_DIM].astype(BF16)
            qf = q_ref[0, g].astype(F32)
            s = jnp.sum(qf * kn.astype(F32), axis=-1, keepdims=True) * scale + biasn_ref[0]
            m_old = m_scr[g]
            m_new = jnp.maximum(m_old, s)
            alpha = jnp.exp(m_old - m_new)
            p = jnp.exp(s - m_new)
            l = alpha * l_scr[g] + p
            acc = alpha * acc_scr[g] + p.astype(BF16).astype(F32) * vn.astype(F32)
            o_ref[0, g] = acc / l


def _attn_sample(pt_flat, q4, bias3, bias_new, k_new, v_new, cache_k, cache_v, layer, n_pages):
    db, _, rows, _ = q4.shape
    per_step = min(n_pages, ATTN_PAGES_PER_STEP)
    assert n_pages % per_step == 0
    page_block = (1, 1, PAGE_SIZE * N_KV_HEADS, HEAD_DIM)
    pages = _page_specs(layer, page_block, n_pages, per_step)
    grid_spec = pltpu.PrefetchScalarGridSpec(
        num_scalar_prefetch=1,
        grid=(db, n_pages // per_step),
        in_specs=[pl.BlockSpec((1, N_KV_HEADS, rows, HEAD_DIM), lambda b, c, pt: (b, 0, 0, 0)),
                  pl.BlockSpec((1, 1, per_step * PAGE_SIZE), lambda b, c, pt: (b, 0, c)),
                  pl.BlockSpec((1, 1, 1), lambda b, c, pt: (b, 0, 0)),
                  pl.BlockSpec((1, 1, KV_WIDTH), lambda b, c, pt: (b, 0, 0)),
                  pl.BlockSpec((1, 1, KV_WIDTH), lambda b, c, pt: (b, 0, 0))] + pages + pages,
        out_specs=pl.BlockSpec((1, N_KV_HEADS, rows, HEAD_DIM), lambda b, c, pt: (b, 0, 0, 0)),
        scratch_shapes=[pltpu.VMEM((N_KV_HEADS, rows, 1), F32),
                        pltpu.VMEM((N_KV_HEADS, rows, 1), F32),
                        pltpu.VMEM((N_KV_HEADS, rows, HEAD_DIM), F32)],
    )
    return pl.pallas_call(
        _attn_sample_kernel,
        grid_spec=grid_spec,
        out_shape=jax.ShapeDtypeStruct((db, N_KV_HEADS, rows, HEAD_DIM), F32),
        compiler_params=_params("arbitrary", "arbitrary"),
        name="attn_sample",
    )(pt_flat, q4, bias3, bias_new, k_new, v_new, *([cache_k] * per_step), *([cache_v] * per_step))


def _mix_sample_kernel(x_ref, a_ref, e_ref, wo_ref, pw_ref, ps_ref, g_ref, y_ref):
    group = pw_ref.shape[1]
    n_rows = e_ref.shape[0]
    d_parts = []
    for g, w in enumerate(POOL_WINDOWS):
        cols = slice(g * group, (g + 1) * group)
        ug = e_ref[n_rows - 1, :, cols]
        wsum = ug
        for s in range(1, w):
            wsum = wsum + e_ref[n_rows - 1 - s, :, cols]
        d_parts.append(wsum / float(w) - ug)
    p = _pool_project(d_parts, pw_ref, ps_ref)
    o = _dot(a_ref[...].astype(BF16), wo_ref[0:ATTN_WIDTH, :]) + _dot(p, wo_ref[ATTN_WIDTH:, :])
    y_ref[...] = x_ref[...] + _rms(o, g_ref[...])


def _mix_sample(x, a, e, wo, pw, ps, g):
    return pl.pallas_call(
        _mix_sample_kernel,
        out_shape=jax.ShapeDtypeStruct(x.shape, F32),
        compiler_params=pltpu.CompilerParams(vmem_limit_bytes=V7X_VMEM_LIMIT_BYTES),
        name="mix_sample",
    )(x, a, e, wo, pw, ps, g)


def _ffn_sample_kernel(x_ref, gpre_ref, wg_ref, wv_ref, cwg_ref, cwv_ref, cbg_ref, cbv_ref, sg_ref, sv_ref,
                       wd_ref, gpost_ref, y_ref, hg_ref, hv_ref, xn_scr, acc_scr):
    j = pl.program_id(0)

    @pl.when(j == 0)
    def _():
        xn_scr[...] = _rms(x_ref[...], gpre_ref[...]).astype(BF16)
        acc_scr[...] = jnp.zeros(acc_scr.shape, F32)

    xn = xn_scr[...]
    hg = _dot(xn, wg_ref[...])
    hv = _dot(xn, wv_ref[...])
    hg_ref[...] = hg
    hv_ref[...] = hv

    def conv(h, s_ref, cw_ref, cb_ref):
        return cb_ref[...] + cw_ref[0:1, :] * s_ref[0] + cw_ref[1:2, :] * s_ref[1] + cw_ref[2:3, :] * h

    cg = conv(hg, sg_ref, cwg_ref, cbg_ref)
    cv = conv(hv, sv_ref, cwv_ref, cbv_ref)
    act = (cg * jax.nn.sigmoid(cg) * cv).astype(BF16)
    acc_scr[...] += _dot(act, wd_ref[...])

    @pl.when(j == pl.num_programs(0) - 1)
    def _():
        y_ref[...] = x_ref[...] + _rms(acc_scr[...], gpost_ref[...])


def _ffn_sample(x, gpre, wup, cw, cb, state, wd, gpost, layer, tf):
    db, d = x.shape
    d_ff = wd.shape[1]
    nj = d_ff // tf
    n_state = state.shape[0]
    return pl.pallas_call(
        _ffn_sample_kernel,
        grid=(nj,),
        in_specs=[
            pl.BlockSpec((db, d), lambda j: (0, 0)),
            pl.BlockSpec((1, d), lambda j: (0, 0)),
            pl.BlockSpec((None, d, tf), lambda j: (layer, 0, j)),
            pl.BlockSpec((None, d, tf), lambda j: (layer, 0, j + nj)),
            pl.BlockSpec((CONV_WIDTH, tf), lambda j: (0, j)),
            pl.BlockSpec((CONV_WIDTH, tf), lambda j: (0, j + nj)),
            pl.BlockSpec((1, tf), lambda j: (0, j)),
            pl.BlockSpec((1, tf), lambda j: (0, j + nj)),
            pl.BlockSpec((n_state, db, tf), lambda j: (0, 0, j)),
            pl.BlockSpec((n_state, db, tf), lambda j: (0, 0, j + nj)),
            pl.BlockSpec((None, tf, d), lambda j: (layer, j, 0)),
            pl.BlockSpec((1, d), lambda j: (0, 0)),
        ],
        out_specs=[pl.BlockSpec((db, d), lambda j: (0, 0)),
                   pl.BlockSpec((db, tf), lambda j: (0, j)),
                   pl.BlockSpec((db, tf), lambda j: (0, j))],
        out_shape=[jax.ShapeDtypeStruct((db, d), F32), jax.ShapeDtypeStruct((db, d_ff), F32),
                   jax.ShapeDtypeStruct((db, d_ff), F32)],
        scratch_shapes=[pltpu.VMEM((db, d), BF16), pltpu.VMEM((db, d), F32)],
        compiler_params=_params("arbitrary"),
        name="ffn_sample",
    )(x, gpre, wup, wup, cw, cw, cb, cb, state, state, wd, gpost)


def _split_w_in(w_in_l):
    d = w_in_l.shape[0]
    q_end = ATTN_WIDTH
    k_end = q_end + KV_WIDTH
    v_end = k_end + KV_WIDTH
    iq_end = v_end + N_IDX_HEADS * IDX_DIM
    ik_end = iq_end + IDX_DIM
    iw_end = ik_end + N_IDX_HEADS
    w = w_in_l.astype(BF16)
    w_ik = w[:, iq_end:ik_end]
    zeros = lambda c: jnp.zeros((d, c), BF16)
    idx = jnp.concatenate([w_ik, zeros(V7X_LANES - IDX_DIM), zeros(V7X_LANES - IDX_DIM), w_ik,
                           w[:, ik_end:iw_end], zeros(V7X_LANES - N_IDX_HEADS)], axis=1)
    return {"q": w[:, :q_end], "kv": w[:, q_end:v_end], "iq": w[:, v_end:iq_end], "idx": idx, "u": w[:, iw_end:]}


def kernel(x_prompt, x_sample, cache_k, cache_v, cache_kidx, state_pool, state_conv, page_table, meta_tokens,
           w_in, w_o, pool_w, pool_scale, g_mix_pre, g_mix_post, g_ffn_pre, g_ffn_post, w_up, conv_w, conv_b,
           w_down):
    b, seq, d = x_prompt.shape
    db, ds, _ = x_sample.shape
    assert ds == 1, "the sample kernels handle one new token per sequence"
    depth = w_in.shape[0]
    n_pool = cache_k.shape[1]
    n_pages = page_table.shape[1]
    assert cache_k.shape[2] == PAGE_SIZE
    past = n_pages * PAGE_SIZE
    t_len = seq + N_META
    d_ff = w_down.shape[1]
    pool_width = d - ATTN_WIDTH
    topk_p = min(TOPK_MAX, t_len // 4)
    topk_s = min(TOPK_MAX, (past + ds) // 4)
    assert past >= max(POOL_WINDOWS)

    m_rows = b * t_len
    tm_proj = _row_tile(m_rows, min(TM_PROJ_CAP, t_len))
    tm_ffn = _row_tile(m_rows, min(TM_FFN_CAP, t_len))
    tf = _ff_tile(d_ff)
    sub8 = V7X_SUBLANES_F32
    heads_pad = -(-HEADS_PER_KV // sub8) * sub8

    meta = jnp.broadcast_to(meta_tokens[None].astype(x_prompt.dtype), (b, N_META, d))
    xp = jnp.concatenate([meta, x_prompt], axis=1).reshape(b * t_len, d)
    xs = x_sample.reshape(db, d)
    pt_flat = page_table.reshape(-1).astype(jnp.int32)
    ck = cache_k.reshape(depth, n_pool, PAGE_SIZE * N_KV_HEADS, HEAD_DIM)
    cv = cache_v.reshape(depth, n_pool, PAGE_SIZE * N_KV_HEADS, HEAD_DIM)
    kidx_t = jnp.swapaxes(cache_kidx, 2, 3)

    row2 = lambda a: a.reshape(1, -1)
    outs = {name: [] for name in ("kp", "vp", "ikp", "poolp", "convp", "ks", "vs", "iks", "pools", "convs")}
    wup = w_up.astype(BF16)
    wd = w_down.astype(BF16)
    for l in range(depth):
        w = _split_w_in(w_in[l])
        wo = w_o[l].astype(BF16)
        pw = pool_w[l].astype(BF16)
        ps = row2(pool_scale[l])
        cw = conv_w[l]
        cb = row2(conv_b[l])

        q, k, v, kb, vb, iq, ik, ik2, iw, u = _in_proj(xp, row2(g_mix_pre[l]), w, tm_proj)
        b3 = lambda a: a.reshape(b, t_len, a.shape[-1])
        a = _attn_prompt(b3(q), b3(iq), jnp.swapaxes(b3(iw), 1, 2), b3(kb), b3(vb), b3(ik2), topk_p)
        xp = _mix_prompt(xp, a.reshape(m_rows, ATTN_WIDTH), u, wo, pw, ps, row2(g_mix_post[l]), tm_proj, t_len)
        xp, last_g, last_v = _ffn_prompt(xp, row2(g_ffn_pre[l]), wup, cw, cb, wd, row2(g_ffn_post[l]), l,
                                         tm_ffn, tf, t_len)
        outs["kp"].append(k.reshape(b, t_len, N_KV_HEADS, HEAD_DIM))
        outs["vp"].append(v.reshape(b, t_len, N_KV_HEADS, HEAD_DIM))
        outs["ikp"].append(ik.reshape(b, t_len, IDX_DIM))
        outs["poolp"].append(u.reshape(b, t_len, pool_width)[:, t_len - POOL_BUF:])
        tail = slice(CONV_HALO - (CONV_WIDTH - 1), CONV_HALO)
        if t_len % tm_ffn == 0:
            per_seq = t_len // tm_ffn
            seq_last = lambda h: h[per_seq - 1::per_seq, tail]
        else:
            end_tiles = [((s + 1) * t_len - 1) // tm_ffn for s in range(b)]
            seq_last = lambda h: jnp.stack([h[i, tail] for i in end_tiles])
        outs["convp"].append(jnp.concatenate([seq_last(last_g), seq_last(last_v)], axis=-1))

        q, k, v, _, _, iq, ik, _, iw, u = _in_proj(xs, row2(g_mix_pre[l]), w, db)
        iq3 = iq.reshape(db, N_IDX_HEADS, IDX_DIM)
        new_page = jnp.pad(ik[:, :, None], ((0, 0), (0, 0), (0, PAGE_SIZE - 1)))
        scores, scores_new = _idx_sample(pt_flat, iq3, iw.reshape(db, N_IDX_HEADS, 1), new_page, kidx_t, l, n_pages)
        bias, bias_new = _select_sample(scores.reshape(db, past), scores_new.reshape(db, PAGE_SIZE), topk_s)
        q4 = jnp.pad(q.reshape(db, N_KV_HEADS, HEADS_PER_KV, HEAD_DIM),
                     ((0, 0), (0, 0), (0, heads_pad - HEADS_PER_KV), (0, 0)))
        a = _attn_sample(pt_flat, q4, bias.reshape(db, 1, past), bias_new.reshape(db, 1, 1),
                         k.reshape(db, 1, KV_WIDTH), v.reshape(db, 1, KV_WIDTH), ck, cv, l, n_pages)
        a = a[:, :, :HEADS_PER_KV].reshape(db, ATTN_WIDTH)
        e = jnp.concatenate([jnp.swapaxes(state_pool[l], 0, 1).astype(u.dtype), u[None]], axis=0)
        xs = _mix_sample(xs, a, e, wo, pw, ps, row2(g_mix_post[l]))
        xs, hg, hv = _ffn_sample(xs, row2(g_ffn_pre[l]), wup, cw, cb, jnp.swapaxes(state_conv[l], 0, 1), wd,
                                 row2(g_ffn_post[l]), l, tf)
        outs["ks"].append(k.reshape(db, ds, N_KV_HEADS, HEAD_DIM))
        outs["vs"].append(v.reshape(db, ds, N_KV_HEADS, HEAD_DIM))
        outs["iks"].append(ik.reshape(db, ds, IDX_DIM))
        outs["pools"].append(jnp.concatenate([state_pool[l].astype(u.dtype), u[:, None, :]], axis=1)[:, 1:])
        h = jnp.concatenate([hg, hv], axis=-1)[:, None, :]
        outs["convs"].append(jnp.concatenate([state_conv[l].astype(h.dtype), h], axis=1)[:, -(CONV_WIDTH - 1):])

    y_prompt = xp.reshape(b, t_len, d)[:, N_META:]
    y_sample = xs.reshape(db, ds, d)
    stack = lambda name: jnp.stack(outs[name])
    return (y_prompt, y_sample, stack("kp"), stack("vp"), stack("ikp"), stack("poolp"), stack("convp"),
            stack("ks"), stack("vs"), stack("iks"), stack("pools"), stack("convs"))
```

```python
import functools

import jax
import jax.numpy as jnp
from jax import lax
from jax.experimental import pallas as pl
from jax.experimental.pallas import tpu as pltpu

N_META = 16
N_HEADS = 8
N_KV_HEADS = 2
HEAD_DIM = 128
N_IDX_HEADS = 16
IDX_DIM = 64
TOPK_MAX = 256
POOL_WINDOWS = (2, 4, 8, 16)
CONV_WIDTH = 3
PAGE_SIZE = 128
Q_BLOCK = 256
RMS_EPS = 1e-6

ATTN_WIDTH = N_HEADS * HEAD_DIM
KV_WIDTH = N_KV_HEADS * HEAD_DIM
HEADS_PER_KV = N_HEADS // N_KV_HEADS
POOL_BUF = max(POOL_WINDOWS) - 1

V7X_LANES = 128
V7X_SUBLANES_F32 = 8
V7X_SUBLANES_BF16 = 16
V7X_VMEM_LIMIT_BYTES = 56 * 1024 * 1024

INT_MIN = -(2 ** 31)
F32 = jnp.float32
BF16 = jnp.bfloat16

CONV_HALO = V7X_SUBLANES_F32
POOL_HALO = 2 * V7X_SUBLANES_F32
IDX_PAGES_PER_STEP = 64
ATTN_PAGES_PER_STEP = 64
KEY_CHUNK = 256
TM_PROJ_CAP = 384
TM_FFN_CAP = 688


def _params(*sem):
    return pltpu.CompilerParams(dimension_semantics=sem, vmem_limit_bytes=V7X_VMEM_LIMIT_BYTES)


def _row_tile(seq_len, cap):
    best = None
    for d in range(V7X_SUBLANES_BF16, min(seq_len, cap) + 1, V7X_SUBLANES_BF16):
        if seq_len % d == 0:
            best = d
    assert best is not None, seq_len
    return best


def _ff_tile(d_ff, cap=512):
    best = None
    for d in range(V7X_LANES, min(d_ff, cap) + 1, V7X_LANES):
        if d_ff % d == 0:
            best = d
    assert best is not None, d_ff
    return best


def _rms(x, g):
    return x * lax.rsqrt(jnp.mean(x * x, axis=-1, keepdims=True) + RMS_EPS) * g


def _dot(a, b):
    return jnp.dot(a, b, preferred_element_type=F32)


def _dot_t(a, b):
    return lax.dot_general(a, b, (((1,), (1,)), ((), ())), preferred_element_type=F32)


def _const_spec(shape):
    n = len(shape)
    return pl.BlockSpec(shape, lambda *_: (0,) * n, pipeline_mode=pl.Buffered(1))


def _in_proj_kernel(x_ref, g_ref, wq_ref, wkv_ref, wiq_ref, widx_ref, wu_ref,
                    q_ref, k_ref, v_ref, kb_ref, vb_ref, iq_ref, ik_ref, ik2_ref, iw_ref, u_ref):
    hn = _rms(x_ref[...], g_ref[...]).astype(BF16)
    q_ref[...] = _dot(hn, wq_ref[...]).astype(BF16)
    kv = _dot(hn, wkv_ref[...])
    tm = x_ref.shape[0]
    for out_ref, c0 in ((k_ref, 0), (v_ref, KV_WIDTH)):
        for g in range(N_KV_HEADS):
            out_ref[pl.ds(g, tm, stride=N_KV_HEADS), :] = kv[:, c0 + g * HEAD_DIM:c0 + (g + 1) * HEAD_DIM]
    kb_ref[...] = kv[:, :KV_WIDTH].astype(BF16)
    vb_ref[...] = kv[:, KV_WIDTH:].astype(BF16)
    iq_ref[...] = _dot(hn, wiq_ref[...]).astype(BF16)
    idx = _dot(hn, widx_ref[...])
    ik_ref[...] = idx[:, :IDX_DIM]
    ik2_ref[...] = idx[:, :2 * V7X_LANES].astype(BF16)
    iw_ref[...] = idx[:, 2 * V7X_LANES:2 * V7X_LANES + N_IDX_HEADS]
    u_ref[...] = _dot(hn, wu_ref[...])


def _in_proj(x, g, w, tm):
    m, d = x.shape
    pool_width = w["u"].shape[1]
    row = lambda c: pl.BlockSpec((tm, c), lambda i: (i, 0))
    outs = [
        (1, ATTN_WIDTH, BF16), (N_KV_HEADS, HEAD_DIM, F32), (N_KV_HEADS, HEAD_DIM, F32), (1, KV_WIDTH, BF16),
        (1, KV_WIDTH, BF16), (1, N_IDX_HEADS * IDX_DIM, BF16), (1, IDX_DIM, F32), (1, 2 * V7X_LANES, BF16),
        (1, N_IDX_HEADS, F32), (1, pool_width, F32),
    ]
    return pl.pallas_call(
        _in_proj_kernel,
        grid=(m // tm,),
        in_specs=[row(d), _const_spec((1, d)), _const_spec(w["q"].shape), _const_spec(w["kv"].shape),
                  _const_spec(w["iq"].shape), _const_spec(w["idx"].shape), _const_spec(w["u"].shape)],
        out_specs=[pl.BlockSpec((r * tm, c), lambda i: (i, 0)) for r, c, _ in outs],
        out_shape=[jax.ShapeDtypeStruct((r * m, c), dt) for r, c, dt in outs],
        compiler_params=_params("arbitrary"),
        name="in_proj",
    )(x, g, w["q"], w["kv"], w["iq"], w["idx"], w["u"])


def _sort_key(x):
    bits = lax.bitcast_convert_type(x, jnp.int32)
    return bits ^ (lax.shift_right_arithmetic(bits, 31) & jnp.int32(0x7FFFFFFF))


def _kth_largest_key(count_ge, shape, k):
    def body(it, state):
        t, cnt_t = state
        cand = t | lax.shift_left(jnp.int32(1), 31 - it)
        cnt = count_ge(cand ^ jnp.int32(INT_MIN))
        take = cnt >= k
        return jnp.where(take, cand, t), jnp.where(take, cnt, cnt_t)

    t, cnt_t = lax.fori_loop(0, 32, body, (jnp.zeros(shape, jnp.int32), jnp.zeros(shape, F32)))
    return t ^ jnp.int32(INT_MIN), cnt_t


def _tie_rank_matrix(n, lower):
    r = lax.broadcasted_iota(jnp.int32, (n, n), 0)
    c = lax.broadcasted_iota(jnp.int32, (n, n), 1)
    return jnp.where(r >= c if lower else r <= c, 1.0, 0.0).astype(BF16)


def _attn_prompt_kernel(q_ref, iq_ref, iwt_ref, kb_ref, vb_ref, ik2_ref, o_ref,
                        k_scr, v_scr, ik_scr, key_scr, thr_scr, cnt_scr, bias_scr, s_scr, m_scr, l_scr, acc_scr,
                        *, topk):
    i = pl.program_id(1)
    t_len = kb_ref.shape[1]
    n_chunks, ch, _ = key_scr.shape
    tk = n_chunks * ch
    halves = ch // V7X_LANES
    rows = HEADS_PER_KV * Q_BLOCK
    nc = ((i + 1) * Q_BLOCK + ch - 1) // ch

    @pl.when(i == 0)
    def _():
        k_scr[0:t_len, :] = kb_ref[0]
        v_scr[0:t_len, :] = vb_ref[0]
        ik_scr[0, 0:t_len, :] = ik2_ref[0, :, 0:V7X_LANES]
        ik_scr[1, 0:t_len, :] = ik2_ref[0, :, V7X_LANES:]
        if tk > t_len:
            k_scr[t_len:tk, :] = jnp.zeros((tk - t_len, k_scr.shape[1]), BF16)
            v_scr[t_len:tk, :] = jnp.zeros((tk - t_len, v_scr.shape[1]), BF16)
            ik_scr[0, t_len:tk, :] = jnp.zeros((tk - t_len, V7X_LANES), BF16)
            ik_scr[1, t_len:tk, :] = jnp.zeros((tk - t_len, V7X_LANES), BF16)

    iw = iwt_ref[0] * (IDX_DIM ** -0.5 * N_IDX_HEADS ** -0.5)
    iq2 = [jnp.concatenate([iq_ref[0, :, (2 * jj + e) * V7X_LANES:(2 * jj + e + 1) * V7X_LANES] for e in range(2)],
                           axis=0) for jj in range(N_IDX_HEADS // 4)]
    qpos = i * Q_BLOCK + lax.broadcasted_iota(jnp.int32, (1, Q_BLOCK), 1)

    def over_chunks(chunk_fn):
        def pair(p, carry):
            chunk_fn(2 * p)
            chunk_fn(2 * p + 1)
            return carry

        lax.fori_loop(0, lax.shift_right_logical(nc, 1), pair, 0)

        @pl.when((nc & 1) == 1)
        def _():
            chunk_fn(nc - 1)

    def index_chunk(c):
        r0 = pl.multiple_of(c * ch, ch)
        keys = jnp.concatenate([ik_scr[0, pl.ds(r0, ch), :], ik_scr[1, pl.ds(r0, ch), :]], axis=0)
        acc = jnp.zeros((ch, Q_BLOCK), F32)
        for jj in range(N_IDX_HEADS // 4):
            s = _dot_t(keys, iq2[jj])
            for e in range(2):
                for half in range(2):
                    h = 2 * (2 * jj + e) + half
                    blk = s[half * ch:(half + 1) * ch, e * Q_BLOCK:(e + 1) * Q_BLOCK]
                    acc = acc + jnp.maximum(blk, 0.0) * iw[h:h + 1, :]
        kpos = r0 + lax.broadcasted_iota(jnp.int32, (ch, 1), 0)
        key_scr[c] = jnp.where(kpos <= qpos, _sort_key(acc), jnp.int32(INT_MIN))

    over_chunks(index_chunk)

    for n_static in range(1, n_chunks + 1):
        @pl.when(nc == n_static)
        def _(n_static=n_static):
            def count_ge(t):
                part = jnp.zeros((V7X_SUBLANES_F32, Q_BLOCK), F32)
                for c in range(n_static):
                    hit = jnp.where(key_scr[c] >= t, 1.0, 0.0)
                    part = part + jnp.sum(hit.reshape(ch // V7X_SUBLANES_F32, V7X_SUBLANES_F32, Q_BLOCK), axis=0)
                return jnp.sum(part, axis=0, keepdims=True)

            t, cnt_t = _kth_largest_key(count_ge, (1, Q_BLOCK), topk)
            thr_scr[...] = jnp.broadcast_to(jnp.maximum(t, jnp.int32(INT_MIN + 1)), thr_scr.shape)
            cnt_scr[...] = jnp.broadcast_to(cnt_t, cnt_scr.shape)

    thr = thr_scr[0:1, :]
    tied = jnp.max(jnp.where(cnt_scr[0:1, :] > topk, 1.0, 0.0)) > 0.0

    def store_bias(c, bias_t):
        for r in range(halves):
            bias_scr[c, :, r * V7X_LANES:(r + 1) * V7X_LANES] = bias_t[r * V7X_LANES:(r + 1) * V7X_LANES, :].T

    @pl.when(jnp.logical_not(tied))
    def _():
        over_chunks(lambda c: store_bias(c, jnp.where(key_scr[c] >= thr, 0.0, -jnp.inf)))

    @pl.when(tied)
    def _():
        def count_above(c, acc):
            return acc + jnp.sum(jnp.where(key_scr[c] > thr, 1.0, 0.0), axis=0, keepdims=True)

        room = topk - lax.fori_loop(0, nc, count_above, jnp.zeros((1, Q_BLOCK), F32))
        rank = _tie_rank_matrix(ch, lower=True)

        def tie_chunk(c, seen):
            key = key_scr[c]
            eq = jnp.where(key == thr, 1.0, 0.0)
            pos = _dot(rank, eq.astype(BF16)) + seen
            keep = jnp.where(key > thr, 1.0, jnp.where(pos <= room, eq, 0.0))
            store_bias(c, jnp.where(keep > 0.0, 0.0, -jnp.inf))
            return seen + jnp.sum(eq, axis=0, keepdims=True)

        lax.fori_loop(0, nc, tie_chunk, jnp.zeros((1, Q_BLOCK), F32))

    scale = HEAD_DIM ** -0.5
    gcols = [slice(g * HEAD_DIM, (g + 1) * HEAD_DIM) for g in range(N_KV_HEADS)]
    qgs = [jnp.concatenate([q_ref[0, :, (g * HEADS_PER_KV + r) * HEAD_DIM:(g * HEADS_PER_KV + r + 1) * HEAD_DIM]
                            for r in range(HEADS_PER_KV)], axis=0) for g in range(N_KV_HEADS)]
    lane_tiles = lambda x: [x[:, r * V7X_LANES:(r + 1) * V7X_LANES] for r in range(halves)]

    m_scr[...] = jnp.full(m_scr.shape, -jnp.inf, F32)

    def score_chunk(c):
        r0 = pl.multiple_of(c * ch, ch)
        bias = bias_scr[c][None]
        for g in range(N_KV_HEADS):
            s = _dot_t(qgs[g], k_scr[pl.ds(r0, ch), gcols[g]])
            s = (s.reshape(HEADS_PER_KV, Q_BLOCK, ch) * scale + bias).reshape(rows, ch)
            s_scr[g, c] = s
            m = m_scr[g]
            for tile in lane_tiles(s):
                m = jnp.maximum(m, tile)
            m_scr[g] = m

    over_chunks(score_chunk)
    for g in range(N_KV_HEADS):
        m_scr[g] = jnp.broadcast_to(jnp.max(m_scr[g], axis=1, keepdims=True), m_scr.shape[1:])
    l_scr[...] = jnp.zeros(l_scr.shape, F32)
    acc_scr[...] = jnp.zeros(acc_scr.shape, F32)

    def value_chunk(c):
        r0 = pl.multiple_of(c * ch, ch)
        for g in range(N_KV_HEADS):
            m = m_scr[g]
            p_tiles = [jnp.exp(tile - m) for tile in lane_tiles(s_scr[g, c])]
            l = l_scr[g]
            for tile in p_tiles:
                l = l + tile
            l_scr[g] = l
            p = jnp.concatenate(p_tiles, axis=1).astype(BF16)
            acc_scr[g] += _dot(p, v_scr[pl.ds(r0, ch), gcols[g]])

    over_chunks(value_chunk)
    for g in range(N_KV_HEADS):
        o = acc_scr[g] / jnp.sum(l_scr[g], axis=1, keepdims=True)
        for r in range(HEADS_PER_KV):
            h = g * HEADS_PER_KV + r
            o_ref[0, :, h * HEAD_DIM:(h + 1) * HEAD_DIM] = o[r * Q_BLOCK:(r + 1) * Q_BLOCK].astype(o_ref.dtype)


def _attn_prompt(q, iq, iwt, kb, vb, ik2, topk):
    b, t_len, _ = q.shape
    n_blk = pl.cdiv(t_len, Q_BLOCK)
    n_chunks = pl.cdiv(n_blk * Q_BLOCK, KEY_CHUNK)
    tk = n_chunks * KEY_CHUNK
    rows = HEADS_PER_KV * Q_BLOCK
    qblk = lambda c: pl.BlockSpec((1, Q_BLOCK, c), lambda bi, i: (bi, i, 0))
    full = lambda c: pl.BlockSpec((1, t_len, c), lambda bi, i: (bi, 0, 0))
    return pl.pallas_call(
        functools.partial(_attn_prompt_kernel, topk=topk),
        grid=(b, n_blk),
        in_specs=[qblk(ATTN_WIDTH), qblk(N_IDX_HEADS * IDX_DIM),
                  pl.BlockSpec((1, N_IDX_HEADS, Q_BLOCK), lambda bi, i: (bi, 0, i)),
                  full(KV_WIDTH), full(KV_WIDTH), full(2 * V7X_LANES)],
        out_specs=qblk(ATTN_WIDTH),
        out_shape=jax.ShapeDtypeStruct((b, t_len, ATTN_WIDTH), BF16),
        scratch_shapes=[pltpu.VMEM((tk, KV_WIDTH), BF16), pltpu.VMEM((tk, KV_WIDTH), BF16),
                        pltpu.VMEM((2, tk, V7X_LANES), BF16),
                        pltpu.VMEM((n_chunks, KEY_CHUNK, Q_BLOCK), jnp.int32),
                        pltpu.VMEM((V7X_SUBLANES_F32, Q_BLOCK), jnp.int32),
                        pltpu.VMEM((V7X_SUBLANES_F32, Q_BLOCK), F32),
                        pltpu.VMEM((n_chunks, Q_BLOCK, KEY_CHUNK), F32),
                        pltpu.VMEM((N_KV_HEADS, n_chunks, rows, KEY_CHUNK), F32),
                        pltpu.VMEM((N_KV_HEADS, rows, V7X_LANES), F32),
                        pltpu.VMEM((N_KV_HEADS, rows, V7X_LANES), F32),
                        pltpu.VMEM((N_KV_HEADS, rows, HEAD_DIM), F32)],
        compiler_params=_params("arbitrary", "arbitrary"),
        name="attn_prompt",
    )(q, iq, iwt, kb, vb, ik2)


def _pool_project(d_parts, pw_ref, ps_ref):
    group = d_parts[0].shape[1]
    outs = []
    for g, d in enumerate(d_parts):
        outs.append(_dot(d.astype(BF16), pw_ref[g]) * ps_ref[:, g * group:(g + 1) * group])
    return jnp.concatenate(outs, axis=1).astype(BF16)


def _pos_in_seq(i, tm, seq_len):
    t = (i * tm) % seq_len + lax.broadcasted_iota(jnp.int32, (tm, 1), 0)
    return jnp.where(t >= seq_len, t - seq_len, t)


def _mix_prompt_kernel(x_ref, a_ref, u_ref, wo_ref, pw_ref, ps_ref, g_ref, gnext_ref, y_ref, yn_ref, e_scr, *,
                       seq_len):
    i = pl.program_id(0)
    tm = x_ref.shape[0]
    group = pw_ref.shape[1]

    @pl.when(i == 0)
    def _():
        e_scr[0:POOL_HALO, :] = jnp.zeros((POOL_HALO, e_scr.shape[1]), F32)

    u = u_ref[...]
    e_scr[POOL_HALO:POOL_HALO + tm, :] = u
    t = _pos_in_seq(i, tm, seq_len)
    in_seq = [None] + [(t >= s).astype(F32) for s in range(1, max(POOL_WINDOWS))]
    d_parts = []
    for g, w in enumerate(POOL_WINDOWS):
        cols = slice(g * group, (g + 1) * group)
        ug = u[:, cols]
        wsum = ug
        for s in range(1, w):
            wsum = wsum + in_seq[s] * e_scr[POOL_HALO - s:POOL_HALO - s + tm, cols]
        cnt = jnp.minimum(t + 1, w).astype(F32)
        d_parts.append(wsum / cnt - ug)
    e_scr[0:POOL_HALO, :] = u[tm - POOL_HALO:, :]
    p = _pool_project(d_parts, pw_ref, ps_ref)
    o = _dot(a_ref[...], wo_ref[0:ATTN_WIDTH, :]) + _dot(p, wo_ref[ATTN_WIDTH:, :])
    y = x_ref[...] + _rms(o, g_ref[...])
    y_ref[...] = y
    yn_ref[...] = _rms(y, gnext_ref[...]).astype(BF16)


def _mix_prompt(x, a, u, wo, pw, ps, g, g_next, tm, seq_len):
    m, d = x.shape
    pool_width = u.shape[1]
    assert POOL_HALO <= tm <= seq_len
    row = lambda c: pl.BlockSpec((tm, c), lambda i: (i, 0))
    return pl.pallas_call(
        functools.partial(_mix_prompt_kernel, seq_len=seq_len),
        grid=(m // tm,),
        in_specs=[row(d), row(ATTN_WIDTH), row(pool_width), _const_spec(wo.shape), _const_spec(pw.shape),
                  _const_spec((1, pool_width)), _const_spec((1, d)), _const_spec((1, d))],
        out_specs=[row(d), row(d)],
        out_shape=[jax.ShapeDtypeStruct((m, d), F32), jax.ShapeDtypeStruct((m, d), BF16)],
        scratch_shapes=[pltpu.VMEM((POOL_HALO + tm, pool_width), F32)],
        compiler_params=_params("arbitrary"),
        name="mix_prompt",
    )(x, a, u, wo, pw, ps, g, g_next)


def _ffn_up_kernel(xn_ref, wg_ref, wv_ref, cwg_ref, cwv_ref, cbg_ref, cbv_ref,
                   act_ref, lastg_ref, lastv_ref, hg_scr, hv_scr, carryg_scr, carryv_scr, *, seq_len):
    i = pl.program_id(1)
    tm = xn_ref.shape[0]

    @pl.when(i == 0)
    def _():
        carryg_scr[...] = jnp.zeros(carryg_scr.shape, F32)
        carryv_scr[...] = jnp.zeros(carryv_scr.shape, F32)

    t = _pos_in_seq(i, tm, seq_len)
    in_seq1 = (t >= 1).astype(F32)
    in_seq2 = (t >= 2).astype(F32)
    seq_end = jnp.minimum(seq_len - (i * tm) % seq_len, tm)
    last_row0 = pl.multiple_of(seq_end, CONV_HALO)

    xn = xn_ref[...]
    acts = []
    for w_ref, cw_ref, cb_ref, h_scr, carry_scr, last_ref in (
            (wg_ref, cwg_ref, cbg_ref, hg_scr, carryg_scr, lastg_ref),
            (wv_ref, cwv_ref, cbv_ref, hv_scr, carryv_scr, lastv_ref)):
        h = _dot(xn, w_ref[...])
        h_scr[0:CONV_HALO, :] = carry_scr[...]
        h_scr[CONV_HALO:CONV_HALO + tm, :] = h
        carry_scr[...] = h[tm - CONV_HALO:, :]
        last_ref[0] = h_scr[pl.ds(last_row0, CONV_HALO), :]
        acts.append(cb_ref[...]
                    + cw_ref[0:1, :] * (in_seq2 * h_scr[CONV_HALO - 2:CONV_HALO - 2 + tm, :])
                    + cw_ref[1:2, :] * (in_seq1 * h_scr[CONV_HALO - 1:CONV_HALO - 1 + tm, :])
                    + cw_ref[2:3, :] * h)
    cg, cv = acts
    act_ref[...] = (cg * jax.nn.sigmoid(cg) * cv).astype(BF16)


def _ffn_up(xn, wup, cw, cb, layer, tm, tf, seq_len):
    m, d = xn.shape
    d_ff = wup.shape[2] // 2
    nj = d_ff // tf
    n_tiles = m // tm
    assert CONV_HALO <= tm <= seq_len and seq_len % CONV_HALO == 0 and tm % CONV_HALO == 0
    return pl.pallas_call(
        functools.partial(_ffn_up_kernel, seq_len=seq_len),
        grid=(nj, n_tiles),
        in_specs=[
            pl.BlockSpec((tm, d), lambda j, i: (i, 0)),
            pl.BlockSpec((None, d, tf), lambda j, i: (layer, 0, j)),
            pl.BlockSpec((None, d, tf), lambda j, i: (layer, 0, j + nj)),
            pl.BlockSpec((CONV_WIDTH, tf), lambda j, i: (0, j)),
            pl.BlockSpec((CONV_WIDTH, tf), lambda j, i: (0, j + nj)),
            pl.BlockSpec((1, tf), lambda j, i: (0, j)),
            pl.BlockSpec((1, tf), lambda j, i: (0, j + nj)),
        ],
        out_specs=[
            pl.BlockSpec((tm, tf), lambda j, i: (i, j)),
            pl.BlockSpec((1, CONV_HALO, tf), lambda j, i: (i, 0, j)),
            pl.BlockSpec((1, CONV_HALO, tf), lambda j, i: (i, 0, j)),
        ],
        out_shape=[jax.ShapeDtypeStruct((m, d_ff), BF16),
                   jax.ShapeDtypeStruct((n_tiles, CONV_HALO, d_ff), F32),
                   jax.ShapeDtypeStruct((n_tiles, CONV_HALO, d_ff), F32)],
        scratch_shapes=[pltpu.VMEM((CONV_HALO + tm, tf), F32), pltpu.VMEM((CONV_HALO + tm, tf), F32),
                        pltpu.VMEM((CONV_HALO, tf), F32), pltpu.VMEM((CONV_HALO, tf), F32)],
        compiler_params=_params("arbitrary", "arbitrary"),
        name="ffn_up",
    )(xn, wup, wup, cw, cw, cb, cb)


def _ffn_down_kernel(act_ref, x_ref, wd_ref, g_ref, y_ref):
    y_ref[...] = x_ref[...] + _rms(_dot(act_ref[...], wd_ref[...]), g_ref[...])


def _ffn_down(act, x, wd, g, layer, tm):
    m, d = x.shape
    d_ff = act.shape[1]
    row = lambda c: pl.BlockSpec((tm, c), lambda i: (i, 0))
    return pl.pallas_call(
        _ffn_down_kernel,
        grid=(m // tm,),
        in_specs=[row(d_ff), row(d),
                  pl.BlockSpec((None, d_ff, d), lambda i: (layer, 0, 0), pipeline_mode=pl.Buffered(1)),
                  _const_spec((1, d))],
        out_specs=row(d),
        out_shape=jax.ShapeDtypeStruct((m, d), F32),
        compiler_params=_params("arbitrary"),
        name="ffn_down",
    )(act, x, wd, g)


def _idx_sample_kernel(pt_ref, iq_ref, iw_ref, new_ref, *refs):
    del pt_ref
    page_refs, o_ref, onew_ref = refs[:-2], refs[-2], refs[-1]
    iq = iq_ref[0]
    iw = iw_ref[0] * (IDX_DIM ** -0.5 * N_IDX_HEADS ** -0.5)

    def scores(keys_t):
        s = _dot(iq, keys_t.astype(BF16))
        return jnp.sum(jnp.maximum(s, 0.0) * iw, axis=0, keepdims=True)

    o_ref[0] = scores(jnp.concatenate([r[0, 0] for r in page_refs], axis=1))
    onew_ref[0] = scores(new_ref[0])


def _page_specs(layer, block, n_pages, per_step):
    def spec(p):
        def index(b, c, pt):
            return (layer, pt[b * n_pages + c * per_step + p]) + (0,) * (len(block) - 2)
        return pl.BlockSpec(block, index)
    return [spec(p) for p in range(per_step)]


def _idx_sample(pt_flat, iq3, iw3, new_page, cache_kidx, layer, n_pages):
    db = iq3.shape[0]
    per_step = min(n_pages, IDX_PAGES_PER_STEP)
    assert n_pages % per_step == 0
    grid_spec = pltpu.PrefetchScalarGridSpec(
        num_scalar_prefetch=1,
        grid=(db, n_pages // per_step),
        in_specs=[pl.BlockSpec((1, N_IDX_HEADS, IDX_DIM), lambda b, c, pt: (b, 0, 0)),
                  pl.BlockSpec((1, N_IDX_HEADS, 1), lambda b, c, pt: (b, 0, 0)),
                  pl.BlockSpec((1, IDX_DIM, PAGE_SIZE), lambda b, c, pt: (b, 0, 0))]
                 + _page_specs(layer, (1, 1, IDX_DIM, PAGE_SIZE), n_pages, per_step),
        out_specs=[pl.BlockSpec((1, 1, per_step * PAGE_SIZE), lambda b, c, pt: (b, 0, c)),
                   pl.BlockSpec((1, 1, PAGE_SIZE), lambda b, c, pt: (b, 0, 0))],
    )
    return pl.pallas_call(
        _idx_sample_kernel,
        grid_spec=grid_spec,
        out_shape=[jax.ShapeDtypeStruct((db, 1, n_pages * PAGE_SIZE), F32),
                   jax.ShapeDtypeStruct((db, 1, PAGE_SIZE), F32)],
        compiler_params=_params("arbitrary", "arbitrary"),
        name="idx_sample",
    )(pt_flat, iq3, iw3, new_page, *([cache_kidx] * per_step))


def _select_sample_kernel(sc_ref, scn_ref, bias_ref, biasn_ref, key_scr, *, topk):
    db = sc_ref.shape[0]
    key_new = _sort_key(scn_ref[:, 0:1])
    key_scr[...] = _sort_key(sc_ref[...])

    def count_ge(t):
        past = jnp.sum(jnp.where(key_scr[...] >= t, 1.0, 0.0), axis=1, keepdims=True)
        return past + jnp.where(key_new >= t, 1.0, 0.0)

    thr, cnt_t = _kth_largest_key(count_ge, (db, 1), topk)
    tied = jnp.max(jnp.where(cnt_t > topk, 1.0, 0.0)) > 0.0

    @pl.when(jnp.logical_not(tied))
    def _():
        bias_ref[...] = jnp.where(key_scr[...] >= thr, 0.0, -jnp.inf)
        biasn_ref[...] = jnp.where(key_new >= thr, 0.0, -jnp.inf)

    @pl.when(tied)
    def _():
        above = (jnp.sum(jnp.where(key_scr[...] > thr, 1.0, 0.0), axis=1, keepdims=True)
                 + jnp.where(key_new > thr, 1.0, 0.0))
        room = topk - above
        rank = _tie_rank_matrix(V7X_LANES, lower=False)
        seen = jnp.zeros((db, 1), F32)
        for blk in range(sc_ref.shape[1] // V7X_LANES):
            cols = slice(blk * V7X_LANES, (blk + 1) * V7X_LANES)
            key = key_scr[:, cols]
            eq = jnp.where(key == thr, 1.0, 0.0)
            pos = _dot(eq.astype(BF16), rank) + seen
            keep = jnp.where(key > thr, 1.0, jnp.where(pos <= room, eq, 0.0))
            bias_ref[:, cols] = jnp.where(keep > 0.0, 0.0, -jnp.inf)
            seen = seen + jnp.sum(eq, axis=1, keepdims=True)
        keep_new = (key_new > thr) | ((key_new == thr) & (seen + 1.0 <= room))
        biasn_ref[...] = jnp.where(keep_new, 0.0, -jnp.inf)


def _select_sample(scores, scores_new, topk):
    db, past = scores.shape
    return pl.pallas_call(
        functools.partial(_select_sample_kernel, topk=topk),
        out_shape=[jax.ShapeDtypeStruct((db, past), F32), jax.ShapeDtypeStruct((db, 1), F32)],
        scratch_shapes=[pltpu.VMEM((db, past), jnp.int32)],
        compiler_params=pltpu.CompilerParams(vmem_limit_bytes=V7X_VMEM_LIMIT_BYTES),
        name="select_sample",
    )(scores, scores_new)


def _attn_sample_kernel(pt_ref, q_ref, bias_ref, biasn_ref, kn_ref, vn_ref, *refs):
    del pt_ref
    per_step = (len(refs) - 4) // 2
    k_refs = refs[:per_step]
    v_refs = refs[per_step:2 * per_step]
    o_ref, m_scr, l_scr, acc_scr = refs[2 * per_step:]
    c = pl.program_id(1)
    scale = HEAD_DIM ** -0.5

    @pl.when(c == 0)
    def _():
        m_scr[...] = jnp.full(m_scr.shape, -jnp.inf, F32)
        l_scr[...] = jnp.zeros(l_scr.shape, F32)
        acc_scr[...] = jnp.zeros(acc_scr.shape, F32)

    def update(g, s, v):
        m_old = m_scr[g]
        m_new = jnp.maximum(m_old, jnp.max(s, axis=-1, keepdims=True))
        safe = jnp.where(m_new == -jnp.inf, 0.0, m_new)
        alpha = jnp.exp(m_old - safe)
        p = jnp.exp(s - safe)
        l_scr[g] = alpha * l_scr[g] + jnp.sum(p, axis=-1, keepdims=True)
        acc_scr[g] = alpha * acc_scr[g] + _dot(p.astype(BF16), v)
        m_scr[g] = m_new

    bias = bias_ref[0]
    for g in range(N_KV_HEADS):
        kg = jnp.concatenate([r[0, 0, pl.ds(g, PAGE_SIZE, stride=N_KV_HEADS), :] for r in k_refs], axis=0)
        vg = jnp.concatenate([r[0, 0, pl.ds(g, PAGE_SIZE, stride=N_KV_HEADS), :] for r in v_refs], axis=0)
        s = _dot_t(q_ref[0, g], kg.astype(BF16)) * scale + bias
        update(g, s, vg.astype(BF16))

    @pl.when(c == pl.num_programs(1) - 1)
    def _():
        for g in range(N_KV_HEADS):
            kn = kn_ref[0, :, g * HEAD_DIM:(g + 1) * HEAD_DIM].astype(BF16)
            vn = vn_ref[0, :, g * HEAD_DIM:(g + 1) * HEAD_DIM].astype(BF16)
            qf = q_ref[0, g].astype(F32)
            s = jnp.sum(qf * kn.astype(F32), axis=-1, keepdims=True) * scale + biasn_ref[0]
            m_old = m_scr[g]
            m_new = jnp.maximum(m_old, s)
            alpha = jnp.exp(m_old - m_new)
            p = jnp.exp(s - m_new)
            l = alpha * l_scr[g] + p
            acc = alpha * acc_scr[g] + p.astype(BF16).astype(F32) * vn.astype(F32)
            o_ref[0, g] = acc / l


def _attn_sample(pt_flat, q4, bias3, bias_new, k_new, v_new, cache_k, cache_v, layer, n_pages):
    db, _, rows, _ = q4.shape
    per_step = min(n_pages, ATTN_PAGES_PER_STEP)
    assert n_pages % per_step == 0
    page_block = (1, 1, PAGE_SIZE * N_KV_HEADS, HEAD_DIM)
    pages = _page_specs(layer, page_block, n_pages, per_step)
    grid_spec = pltpu.PrefetchScalarGridSpec(
        num_scalar_prefetch=1,
        grid=(db, n_pages // per_step),
        in_specs=[pl.BlockSpec((1, N_KV_HEADS, rows, HEAD_DIM), lambda b, c, pt: (b, 0, 0, 0)),
                  pl.BlockSpec((1, 1, per_step * PAGE_SIZE), lambda b, c, pt: (b, 0, c)),
                  pl.BlockSpec((1, 1, 1), lambda b, c, pt: (b, 0, 0)),
                  pl.BlockSpec((1, 1, KV_WIDTH), lambda b, c, pt: (b, 0, 0)),
                  pl.BlockSpec((1, 1, KV_WIDTH), lambda b, c, pt: (b, 0, 0))] + pages + pages,
        out_specs=pl.BlockSpec((1, N_KV_HEADS, rows, HEAD_DIM), lambda b, c, pt: (b, 0, 0, 0)),
        scratch_shapes=[pltpu.VMEM((N_KV_HEADS, rows, 1), F32),
                        pltpu.VMEM((N_KV_HEADS, rows, 1), F32),
                        pltpu.VMEM((N_KV_HEADS, rows, HEAD_DIM), F32)],
    )
    return pl.pallas_call(
        _attn_sample_kernel,
        grid_spec=grid_spec,
        out_shape=jax.ShapeDtypeStruct((db, N_KV_HEADS, rows, HEAD_DIM), F32),
        compiler_params=_params("arbitrary", "arbitrary"),
        name="attn_sample",
    )(pt_flat, q4, bias3, bias_new, k_new, v_new, *([cache_k] * per_step), *([cache_v] * per_step))


def _mix_sample_kernel(x_ref, a_ref, e_ref, wo_ref, pw_ref, ps_ref, g_ref, y_ref):
    group = pw_ref.shape[1]
    n_rows = e_ref.shape[0]
    d_parts = []
    for g, w in enumerate(POOL_WINDOWS):
        cols = slice(g * group, (g + 1) * group)
        ug = e_ref[n_rows - 1, :, cols]
        wsum = ug
        for s in range(1, w):
            wsum = wsum + e_ref[n_rows - 1 - s, :, cols]
        d_parts.append(wsum / float(w) - ug)
    p = _pool_project(d_parts, pw_ref, ps_ref)
    o = _dot(a_ref[...].astype(BF16), wo_ref[0:ATTN_WIDTH, :]) + _dot(p, wo_ref[ATTN_WIDTH:, :])
    y_ref[...] = x_ref[...] + _rms(o, g_ref[...])


def _mix_sample(x, a, e, wo, pw, ps, g):
    return pl.pallas_call(
        _mix_sample_kernel,
        out_shape=jax.ShapeDtypeStruct(x.shape, F32),
        compiler_params=pltpu.CompilerParams(vmem_limit_bytes=V7X_VMEM_LIMIT_BYTES),
        name="mix_sample",
    )(x, a, e, wo, pw, ps, g)


def _ffn_sample_kernel(x_ref, gpre_ref, wg_ref, wv_ref, cwg_ref, cwv_ref, cbg_ref, cbv_ref, sg_ref, sv_ref,
                       wd_ref, gpost_ref, y_ref, hg_ref, hv_ref, xn_scr, acc_scr):
    j = pl.program_id(0)

    @pl.when(j == 0)
    def _():
        xn_scr[...] = _rms(x_ref[...], gpre_ref[...]).astype(BF16)
        acc_scr[...] = jnp.zeros(acc_scr.shape, F32)

    xn = xn_scr[...]
    hg = _dot(xn, wg_ref[...])
    hv = _dot(xn, wv_ref[...])
    hg_ref[...] = hg
    hv_ref[...] = hv

    def conv(h, s_ref, cw_ref, cb_ref):
        return cb_ref[...] + cw_ref[0:1, :] * s_ref[0] + cw_ref[1:2, :] * s_ref[1] + cw_ref[2:3, :] * h

    cg = conv(hg, sg_ref, cwg_ref, cbg_ref)
    cv = conv(hv, sv_ref, cwv_ref, cbv_ref)
    act = (cg * jax.nn.sigmoid(cg) * cv).astype(BF16)
    acc_scr[...] += _dot(act, wd_ref[...])

    @pl.when(j == pl.num_programs(0) - 1)
    def _():
        y_ref[...] = x_ref[...] + _rms(acc_scr[...], gpost_ref[...])


def _ffn_sample(x, gpre, wup, cw, cb, state, wd, gpost, layer, tf):
    db, d = x.shape
    d_ff = wd.shape[1]
    nj = d_ff // tf
    n_state = state.shape[0]
    return pl.pallas_call(
        _ffn_sample_kernel,
        grid=(nj,),
        in_specs=[
            pl.BlockSpec((db, d), lambda j: (0, 0)),
            pl.BlockSpec((1, d), lambda j: (0, 0)),
            pl.BlockSpec((None, d, tf), lambda j: (layer, 0, j)),
            pl.BlockSpec((None, d, tf), lambda j: (layer, 0, j + nj)),
            pl.BlockSpec((CONV_WIDTH, tf), lambda j: (0, j)),
            pl.BlockSpec((CONV_WIDTH, tf), lambda j: (0, j + nj)),
            pl.BlockSpec((1, tf), lambda j: (0, j)),
            pl.BlockSpec((1, tf), lambda j: (0, j + nj)),
            pl.BlockSpec((n_state, db, tf), lambda j: (0, 0, j)),
            pl.BlockSpec((n_state, db, tf), lambda j: (0, 0, j + nj)),
            pl.BlockSpec((None, tf, d), lambda j: (layer, j, 0)),
            pl.BlockSpec((1, d), lambda j: (0, 0)),
        ],
        out_specs=[pl.BlockSpec((db, d), lambda j: (0, 0)),
                   pl.BlockSpec((db, tf), lambda j: (0, j)),
                   pl.BlockSpec((db, tf), lambda j: (0, j))],
        out_shape=[jax.ShapeDtypeStruct((db, d), F32), jax.ShapeDtypeStruct((db, d_ff), F32),
                   jax.ShapeDtypeStruct((db, d_ff), F32)],
        scratch_shapes=[pltpu.VMEM((db, d), BF16), pltpu.VMEM((db, d), F32)],
        compiler_params=_params("arbitrary"),
        name="ffn_sample",
    )(x, gpre, wup, wup, cw, cw, cb, cb, state, state, wd, gpost)


def _split_w_in(w_in_l):
    d = w_in_l.shape[0]
    q_end = ATTN_WIDTH
    k_end = q_end + KV_WIDTH
    v_end = k_end + KV_WIDTH
    iq_end = v_end + N_IDX_HEADS * IDX_DIM
    ik_end = iq_end + IDX_DIM
    iw_end = ik_end + N_IDX_HEADS
    w = w_in_l.astype(BF16)
    w_ik = w[:, iq_end:ik_end]
    zeros = lambda c: jnp.zeros((d, c), BF16)
    idx = jnp.concatenate([w_ik, zeros(V7X_LANES - IDX_DIM), zeros(V7X_LANES - IDX_DIM), w_ik,
                           w[:, ik_end:iw_end], zeros(V7X_LANES - N_IDX_HEADS)], axis=1)
    return {"q": w[:, :q_end], "kv": w[:, q_end:v_end], "iq": w[:, v_end:iq_end], "idx": idx, "u": w[:, iw_end:]}


def kernel(x_prompt, x_sample, cache_k, cache_v, cache_kidx, state_pool, state_conv, page_table, meta_tokens,
           w_in, w_o, pool_w, pool_scale, g_mix_pre, g_mix_post, g_ffn_pre, g_ffn_post, w_up, conv_w, conv_b,
           w_down):
    b, seq, d = x_prompt.shape
    db, ds, _ = x_sample.shape
    assert ds == 1, "the sample kernels handle one new token per sequence"
    depth = w_in.shape[0]
    n_pool = cache_k.shape[1]
    n_pages = page_table.shape[1]
    assert cache_k.shape[2] == PAGE_SIZE
    past = n_pages * PAGE_SIZE
    t_len = seq + N_META
    d_ff = w_down.shape[1]
    pool_width = d - ATTN_WIDTH
    topk_p = min(TOPK_MAX, t_len // 4)
    topk_s = min(TOPK_MAX, (past + ds) // 4)
    assert past >= max(POOL_WINDOWS)

    m_rows = b * t_len
    tm_proj = _row_tile(m_rows, min(TM_PROJ_CAP, t_len))
    tm_ffn = _row_tile(m_rows, min(TM_FFN_CAP, t_len))
    tf = _ff_tile(d_ff)
    sub8 = V7X_SUBLANES_F32
    heads_pad = -(-HEADS_PER_KV // sub8) * sub8

    meta = jnp.broadcast_to(meta_tokens[None].astype(x_prompt.dtype), (b, N_META, d))
    xp = jnp.concatenate([meta, x_prompt], axis=1).reshape(b * t_len, d)
    xs = x_sample.reshape(db, d)
    pt_flat = page_table.reshape(-1).astype(jnp.int32)
    ck = cache_k.reshape(depth, n_pool, PAGE_SIZE * N_KV_HEADS, HEAD_DIM)
    cv = cache_v.reshape(depth, n_pool, PAGE_SIZE * N_KV_HEADS, HEAD_DIM)
    kidx_t = jnp.swapaxes(cache_kidx, 2, 3)

    row2 = lambda a: a.reshape(1, -1)
    outs = {name: [] for name in ("kp", "vp", "ikp", "poolp", "convp", "ks", "vs", "iks", "pools", "convs")}
    wup = w_up.astype(BF16)
    wd = w_down.astype(BF16)
    for l in range(depth):
        w = _split_w_in(w_in[l])
        wo = w_o[l].astype(BF16)
        pw = pool_w[l].astype(BF16)
        ps = row2(pool_scale[l])
        cw = conv_w[l]
        cb = row2(conv_b[l])

        q, k, v, kb, vb, iq, ik, ik2, iw, u = _in_proj(xp, row2(g_mix_pre[l]), w, tm_proj)
        b3 = lambda a: a.reshape(b, t_len, a.shape[-1])
        a = _attn_prompt(b3(q), b3(iq), jnp.swapaxes(b3(iw), 1, 2), b3(kb), b3(vb), b3(ik2), topk_p)
        xp, xn = _mix_prompt(xp, a.reshape(m_rows, ATTN_WIDTH), u, wo, pw, ps, row2(g_mix_post[l]),
                             row2(g_ffn_pre[l]), tm_proj, t_len)
        act, last_g, last_v = _ffn_up(xn, wup, cw, cb, l, tm_ffn, tf, t_len)
        xp = _ffn_down(act, xp, wd, row2(g_ffn_post[l]), l, tm_proj)
        outs["kp"].append(k.reshape(b, t_len, N_KV_HEADS, HEAD_DIM))
        outs["vp"].append(v.reshape(b, t_len, N_KV_HEADS, HEAD_DIM))
        outs["ikp"].append(ik.reshape(b, t_len, IDX_DIM))
        outs["poolp"].append(u.reshape(b, t_len, pool_width)[:, t_len - POOL_BUF:])
        tail = slice(CONV_HALO - (CONV_WIDTH - 1), CONV_HALO)
        if t_len % tm_ffn == 0:
            per_seq = t_len // tm_ffn
            seq_last = lambda h: h[per_seq - 1::per_seq, tail]
        else:
            end_tiles = [((s + 1) * t_len - 1) // tm_ffn for s in range(b)]
            seq_last = lambda h: jnp.stack([h[i, tail] for i in end_tiles])
        outs["convp"].append(jnp.concatenate([seq_last(last_g), seq_last(last_v)], axis=-1))

        q, k, v, _, _, iq, ik, _, iw, u = _in_proj(xs, row2(g_mix_pre[l]), w, db)
        iq3 = iq.reshape(db, N_IDX_HEADS, IDX_DIM)
        new_page = jnp.pad(ik[:, :, None], ((0, 0), (0, 0), (0, PAGE_SIZE - 1)))
        scores, scores_new = _idx_sample(pt_flat, iq3, iw.reshape(db, N_IDX_HEADS, 1), new_page, kidx_t, l, n_pages)
        bias, bias_new = _select_sample(scores.reshape(db, past), scores_new.reshape(db, PAGE_SIZE), topk_s)
        q4 = jnp.pad(q.reshape(db, N_KV_HEADS, HEADS_PER_KV, HEAD_DIM),
                     ((0, 0), (0, 0), (0, heads_pad - HEADS_PER_KV), (0, 0)))
        a = _attn_sample(pt_flat, q4, bias.reshape(db, 1, past), bias_new.reshape(db, 1, 1),
                         k.reshape(db, 1, KV_WIDTH), v.reshape(db, 1, KV_WIDTH), ck, cv, l, n_pages)
        a = a[:, :, :HEADS_PER_KV].reshape(db, ATTN_WIDTH)
        e = jnp.concatenate([jnp.swapaxes(state_pool[l], 0, 1).astype(u.dtype), u[None]], axis=0)
        xs = _mix_sample(xs, a, e, wo, pw, ps, row2(g_mix_post[l]))
        xs, hg, hv = _ffn_sample(xs, row2(g_ffn_pre[l]), wup, cw, cb, jnp.swapaxes(state_conv[l], 0, 1), wd,
                                 row2(g_ffn_post[l]), l, tf)
        outs["ks"].append(k.reshape(db, ds, N_KV_HEADS, HEAD_DIM))
        outs["vs"].append(v.reshape(db, ds, N_KV_HEADS, HEAD_DIM))
        outs["iks"].append(ik.reshape(db, ds, IDX_DIM))
        outs["pools"].append(jnp.concatenate([state_pool[l].astype(u.dtype), u[:, None, :]], axis=1)[:, 1:])
        h = jnp.concatenate([hg, hv], axis=-1)[:, None, :]
        outs["convs"].append(jnp.concatenate([state_conv[l].astype(h.dtype), h], axis=1)[:, -(CONV_WIDTH - 1):])

    y_prompt = xp.reshape(b, t_len, d)[:, N_META:]
    y_sample = xs.reshape(db, ds, d)
    stack = lambda name: jnp.stack(outs[name])
    return (y_prompt, y_sample, stack("kp"), stack("vp"), stack("ikp"), stack("poolp"), stack("convp"),
            stack("ks"), stack("vs"), stack("iks"), stack("pools"), stack("convs"))
```

```python
import functools

import jax
import jax.numpy as jnp
from jax import lax
from jax.experimental import pallas as pl
from jax.experimental.pallas import tpu as pltpu

N_META = 16
N_HEADS = 8
N_KV_HEADS = 2
HEAD_DIM = 128
N_IDX_HEADS = 16
IDX_DIM = 64
TOPK_MAX = 256
POOL_WINDOWS = (2, 4, 8, 16)
CONV_WIDTH = 3
PAGE_SIZE = 128
Q_BLOCK = 256
RMS_EPS = 1e-6

ATTN_WIDTH = N_HEADS * HEAD_DIM
KV_WIDTH = N_KV_HEADS * HEAD_DIM
HEADS_PER_KV = N_HEADS // N_KV_HEADS
POOL_BUF = max(POOL_WINDOWS) - 1

V7X_LANES = 128
V7X_SUBLANES_F32 = 8
V7X_SUBLANES_BF16 = 16
V7X_VMEM_LIMIT_BYTES = 56 * 1024 * 1024

INT_MIN = -(2 ** 31)
F32 = jnp.float32
BF16 = jnp.bfloat16

CONV_HALO = V7X_SUBLANES_F32
POOL_HALO = 2 * V7X_SUBLANES_F32
IDX_PAGES_PER_STEP = 64
ATTN_PAGES_PER_STEP = 64
KEY_CHUNK = 256
TM_PROJ_CAP = 384
TM_DOWN_CAP = 384
TM_TAIL_CAP = 256
TM_FFN_CAP = 688


def _params(*sem):
    return pltpu.CompilerParams(dimension_semantics=sem, vmem_limit_bytes=V7X_VMEM_LIMIT_BYTES)


def _row_tile(seq_len, cap):
    best = None
    for d in range(V7X_SUBLANES_BF16, min(seq_len, cap) + 1, V7X_SUBLANES_BF16):
        if seq_len % d == 0:
            best = d
    assert best is not None, seq_len
    return best


def _ff_tile(d_ff, cap=512):
    best = None
    for d in range(V7X_LANES, min(d_ff, cap) + 1, V7X_LANES):
        if d_ff % d == 0:
            best = d
    assert best is not None, d_ff
    return best


def _rms(x, g):
    return x * lax.rsqrt(jnp.mean(x * x, axis=-1, keepdims=True) + RMS_EPS) * g


def _dot(a, b):
    return jnp.dot(a, b, preferred_element_type=F32)


def _dot_t(a, b):
    return lax.dot_general(a, b, (((1,), (1,)), ((), ())), preferred_element_type=F32)


def _const_spec(shape):
    n = len(shape)
    return pl.BlockSpec(shape, lambda *_: (0,) * n, pipeline_mode=pl.Buffered(1))


def _stream_specs(x, tm, meta):
    d = x.shape[-1]
    if meta is None:
        return [pl.BlockSpec((tm, d), lambda i: (i, 0))], [x], 0
    n_meta = meta.shape[0]
    per_seq = (x.shape[1] + n_meta) // tm
    assert per_seq * tm == x.shape[1] + n_meta and n_meta % V7X_SUBLANES_BF16 == 0 and tm > n_meta

    def window(i):
        start = jnp.maximum((i % per_seq) * tm - n_meta, 0)
        return i // per_seq, pl.multiple_of(start, V7X_SUBLANES_BF16), 0

    return ([pl.BlockSpec((pl.Element(1), pl.Element(tm), pl.Element(d)), window), _const_spec(meta.shape)],
            [x, meta], per_seq)


def _stream_tile(x_refs, per_seq):
    if len(x_refs) == 1:
        return x_refs[0][...]
    win_ref, meta_ref = x_refs
    win = win_ref[0]
    n_meta = meta_ref.shape[0]
    first = jnp.concatenate([meta_ref[...], win[:win.shape[0] - n_meta]], axis=0)
    return jnp.where(pl.program_id(0) % per_seq == 0, first, win)


def _in_proj_kernel(*refs, per_seq):
    (g_ref, wq_ref, wkv_ref, wiq_ref, widx_ref, wu_ref,
     q_ref, k_ref, v_ref, kb_ref, vb_ref, iq_ref, ik_ref, ik2_ref, iw_ref, u_ref) = refs[-16:]
    x = _stream_tile(refs[:-16], per_seq)
    hn = _rms(x, g_ref[...]).astype(BF16)
    q_ref[...] = _dot(hn, wq_ref[...]).astype(BF16)
    kv = _dot(hn, wkv_ref[...])
    tm = x.shape[0]
    for out_ref, c0 in ((k_ref, 0), (v_ref, KV_WIDTH)):
        for g in range(N_KV_HEADS):
            out_ref[pl.ds(g, tm, stride=N_KV_HEADS), :] = kv[:, c0 + g * HEAD_DIM:c0 + (g + 1) * HEAD_DIM]
    kb_ref[...] = kv[:, :KV_WIDTH].astype(BF16)
    vb_ref[...] = kv[:, KV_WIDTH:].astype(BF16)
    iq_ref[...] = _dot(hn, wiq_ref[...]).astype(BF16)
    idx = _dot(hn, widx_ref[...])
    ik_ref[...] = idx[:, :IDX_DIM]
    ik2_ref[...] = idx[:, :2 * V7X_LANES].astype(BF16)
    iw_ref[...] = idx[:, 2 * V7X_LANES:2 * V7X_LANES + N_IDX_HEADS]
    u_ref[...] = _dot(hn, wu_ref[...])


def _in_proj(x, g, w, tm, meta=None):
    x_specs, x_ops, per_seq = _stream_specs(x, tm, meta)
    d = x.shape[-1]
    m = x.shape[0] if meta is None else x.shape[0] * (x.shape[1] + meta.shape[0])
    pool_width = w["u"].shape[1]
    outs = [
        (1, ATTN_WIDTH, BF16), (N_KV_HEADS, HEAD_DIM, F32), (N_KV_HEADS, HEAD_DIM, F32), (1, KV_WIDTH, BF16),
        (1, KV_WIDTH, BF16), (1, N_IDX_HEADS * IDX_DIM, BF16), (1, IDX_DIM, F32), (1, 2 * V7X_LANES, BF16),
        (1, N_IDX_HEADS, F32), (1, pool_width, F32),
    ]
    return pl.pallas_call(
        functools.partial(_in_proj_kernel, per_seq=per_seq),
        grid=(m // tm,),
        in_specs=x_specs + [_const_spec((1, d)), _const_spec(w["q"].shape), _const_spec(w["kv"].shape),
                            _const_spec(w["iq"].shape), _const_spec(w["idx"].shape), _const_spec(w["u"].shape)],
        out_specs=[pl.BlockSpec((r * tm, c), lambda i: (i, 0)) for r, c, _ in outs],
        out_shape=[jax.ShapeDtypeStruct((r * m, c), dt) for r, c, dt in outs],
        compiler_params=_params("arbitrary"),
        name="in_proj",
    )(*x_ops, g, w["q"], w["kv"], w["iq"], w["idx"], w["u"])


def _sort_key(x):
    bits = lax.bitcast_convert_type(x, jnp.int32)
    return bits ^ (lax.shift_right_arithmetic(bits, 31) & jnp.int32(0x7FFFFFFF))


def _kth_largest_key(count_ge, shape, k):
    def body(it, state):
        t, cnt_t = state
        cand = t | lax.shift_left(jnp.int32(1), 31 - it)
        cnt = count_ge(cand ^ jnp.int32(INT_MIN))
        take = cnt >= k
        return jnp.where(take, cand, t), jnp.where(take, cnt, cnt_t)

    t, cnt_t = lax.fori_loop(0, 32, body, (jnp.zeros(shape, jnp.int32), jnp.zeros(shape, F32)))
    return t ^ jnp.int32(INT_MIN), cnt_t


def _tie_rank_matrix(n, lower):
    r = lax.broadcasted_iota(jnp.int32, (n, n), 0)
    c = lax.broadcasted_iota(jnp.int32, (n, n), 1)
    return jnp.where(r >= c if lower else r <= c, 1.0, 0.0).astype(BF16)


def _attn_prompt_kernel(q_ref, iq_ref, iwt_ref, kb_ref, vb_ref, ik2_ref, o_ref,
                        k_scr, v_scr, ik_scr, key_scr, thr_scr, cnt_scr, bias_scr, s_scr, m_scr, l_scr, acc_scr,
                        *, topk):
    i = pl.program_id(1)
    t_len = kb_ref.shape[1]
    n_chunks, ch, _ = key_scr.shape
    tk = n_chunks * ch
    halves = ch // V7X_LANES
    rows = HEADS_PER_KV * Q_BLOCK
    nc = ((i + 1) * Q_BLOCK + ch - 1) // ch

    @pl.when(i == 0)
    def _():
        k_scr[0:t_len, :] = kb_ref[0]
        v_scr[0:t_len, :] = vb_ref[0]
        ik_scr[0, 0:t_len, :] = ik2_ref[0, :, 0:V7X_LANES]
        ik_scr[1, 0:t_len, :] = ik2_ref[0, :, V7X_LANES:]
        if tk > t_len:
            k_scr[t_len:tk, :] = jnp.zeros((tk - t_len, k_scr.shape[1]), BF16)
            v_scr[t_len:tk, :] = jnp.zeros((tk - t_len, v_scr.shape[1]), BF16)
            ik_scr[0, t_len:tk, :] = jnp.zeros((tk - t_len, V7X_LANES), BF16)
            ik_scr[1, t_len:tk, :] = jnp.zeros((tk - t_len, V7X_LANES), BF16)

    iw = iwt_ref[0] * (IDX_DIM ** -0.5 * N_IDX_HEADS ** -0.5)
    iq2 = [jnp.concatenate([iq_ref[0, :, (2 * jj + e) * V7X_LANES:(2 * jj + e + 1) * V7X_LANES] for e in range(2)],
                           axis=0) for jj in range(N_IDX_HEADS // 4)]
    qpos = i * Q_BLOCK + lax.broadcasted_iota(jnp.int32, (1, Q_BLOCK), 1)

    def over_chunks(chunk_fn):
        def pair(p, carry):
            chunk_fn(2 * p)
            chunk_fn(2 * p + 1)
            return carry

        lax.fori_loop(0, lax.shift_right_logical(nc, 1), pair, 0)

        @pl.when((nc & 1) == 1)
        def _():
            chunk_fn(nc - 1)

    def index_chunk(c):
        r0 = pl.multiple_of(c * ch, ch)
        keys = jnp.concatenate([ik_scr[0, pl.ds(r0, ch), :], ik_scr[1, pl.ds(r0, ch), :]], axis=0)
        acc = jnp.zeros((ch, Q_BLOCK), F32)
        for jj in range(N_IDX_HEADS // 4):
            s = _dot_t(keys, iq2[jj])
            for e in range(2):
                for half in range(2):
                    h = 2 * (2 * jj + e) + half
                    blk = s[half * ch:(half + 1) * ch, e * Q_BLOCK:(e + 1) * Q_BLOCK]
                    acc = acc + jnp.maximum(blk, 0.0) * iw[h:h + 1, :]
        kpos = r0 + lax.broadcasted_iota(jnp.int32, (ch, 1), 0)
        key_scr[c] = jnp.where(kpos <= qpos, _sort_key(acc), jnp.int32(INT_MIN))

    over_chunks(index_chunk)

    for n_static in range(1, n_chunks + 1):
        @pl.when(nc == n_static)
        def _(n_static=n_static):
            def count_ge(t):
                part = jnp.zeros((V7X_SUBLANES_F32, Q_BLOCK), F32)
                for c in range(n_static):
                    hit = jnp.where(key_scr[c] >= t, 1.0, 0.0)
                    part = part + jnp.sum(hit.reshape(ch // V7X_SUBLANES_F32, V7X_SUBLANES_F32, Q_BLOCK), axis=0)
                return jnp.sum(part, axis=0, keepdims=True)

            t, cnt_t = _kth_largest_key(count_ge, (1, Q_BLOCK), topk)
            thr_scr[...] = jnp.broadcast_to(jnp.maximum(t, jnp.int32(INT_MIN + 1)), thr_scr.shape)
            cnt_scr[...] = jnp.broadcast_to(cnt_t, cnt_scr.shape)

    thr = thr_scr[0:1, :]
    tied = jnp.max(jnp.where(cnt_scr[0:1, :] > topk, 1.0, 0.0)) > 0.0

    def store_bias(c, bias_t):
        for r in range(halves):
            bias_scr[c, :, r * V7X_LANES:(r + 1) * V7X_LANES] = bias_t[r * V7X_LANES:(r + 1) * V7X_LANES, :].T

    @pl.when(jnp.logical_not(tied))
    def _():
        over_chunks(lambda c: store_bias(c, jnp.where(key_scr[c] >= thr, 0.0, -jnp.inf)))

    @pl.when(tied)
    def _():
        def count_above(c, acc):
            return acc + jnp.sum(jnp.where(key_scr[c] > thr, 1.0, 0.0), axis=0, keepdims=True)

        room = topk - lax.fori_loop(0, nc, count_above, jnp.zeros((1, Q_BLOCK), F32))
        rank = _tie_rank_matrix(ch, lower=True)

        def tie_chunk(c, seen):
            key = key_scr[c]
            eq = jnp.where(key == thr, 1.0, 0.0)
            pos = _dot(rank, eq.astype(BF16)) + seen
            keep = jnp.where(key > thr, 1.0, jnp.where(pos <= room, eq, 0.0))
            store_bias(c, jnp.where(keep > 0.0, 0.0, -jnp.inf))
            return seen + jnp.sum(eq, axis=0, keepdims=True)

        lax.fori_loop(0, nc, tie_chunk, jnp.zeros((1, Q_BLOCK), F32))

    scale = HEAD_DIM ** -0.5
    gcols = [slice(g * HEAD_DIM, (g + 1) * HEAD_DIM) for g in range(N_KV_HEADS)]
    qgs = [jnp.concatenate([q_ref[0, :, (g * HEADS_PER_KV + r) * HEAD_DIM:(g * HEADS_PER_KV + r + 1) * HEAD_DIM]
                            for r in range(HEADS_PER_KV)], axis=0) for g in range(N_KV_HEADS)]
    lane_tiles = lambda x: [x[:, r * V7X_LANES:(r + 1) * V7X_LANES] for r in range(halves)]

    m_scr[...] = jnp.full(m_scr.shape, -jnp.inf, F32)

    def score_chunk(c):
        r0 = pl.multiple_of(c * ch, ch)
        bias = bias_scr[c][None]
        for g in range(N_KV_HEADS):
            s = _dot_t(qgs[g], k_scr[pl.ds(r0, ch), gcols[g]])
            s = (s.reshape(HEADS_PER_KV, Q_BLOCK, ch) * scale + bias).reshape(rows, ch)
            s_scr[g, c] = s
            m = m_scr[g]
            for tile in lane_tiles(s):
                m = jnp.maximum(m, tile)
            m_scr[g] = m

    over_chunks(score_chunk)
    for g in range(N_KV_HEADS):
        m_scr[g] = jnp.broadcast_to(jnp.max(m_scr[g], axis=1, keepdims=True), m_scr.shape[1:])
    l_scr[...] = jnp.zeros(l_scr.shape, F32)
    acc_scr[...] = jnp.zeros(acc_scr.shape, F32)

    def value_chunk(c):
        r0 = pl.multiple_of(c * ch, ch)
        for g in range(N_KV_HEADS):
            m = m_scr[g]
            p_tiles = [jnp.exp(tile - m) for tile in lane_tiles(s_scr[g, c])]
            l = l_scr[g]
            for tile in p_tiles:
                l = l + tile
            l_scr[g] = l
            p = jnp.concatenate(p_tiles, axis=1).astype(BF16)
            acc_scr[g] += _dot(p, v_scr[pl.ds(r0, ch), gcols[g]])

    over_chunks(value_chunk)
    for g in range(N_KV_HEADS):
        o = acc_scr[g] / jnp.sum(l_scr[g], axis=1, keepdims=True)
        for r in range(HEADS_PER_KV):
            h = g * HEADS_PER_KV + r
            o_ref[0, :, h * HEAD_DIM:(h + 1) * HEAD_DIM] = o[r * Q_BLOCK:(r + 1) * Q_BLOCK].astype(o_ref.dtype)


def _attn_prompt(q, iq, iwt, kb, vb, ik2, topk):
    b, t_len, _ = q.shape
    n_blk = pl.cdiv(t_len, Q_BLOCK)
    n_chunks = pl.cdiv(n_blk * Q_BLOCK, KEY_CHUNK)
    tk = n_chunks * KEY_CHUNK
    rows = HEADS_PER_KV * Q_BLOCK
    qblk = lambda c: pl.BlockSpec((1, Q_BLOCK, c), lambda bi, i: (bi, i, 0))
    full = lambda c: pl.BlockSpec((1, t_len, c), lambda bi, i: (bi, 0, 0))
    return pl.pallas_call(
        functools.partial(_attn_prompt_kernel, topk=topk),
        grid=(b, n_blk),
        in_specs=[qblk(ATTN_WIDTH), qblk(N_IDX_HEADS * IDX_DIM),
                  pl.BlockSpec((1, N_IDX_HEADS, Q_BLOCK), lambda bi, i: (bi, 0, i)),
                  full(KV_WIDTH), full(KV_WIDTH), full(2 * V7X_LANES)],
        out_specs=qblk(ATTN_WIDTH),
        out_shape=jax.ShapeDtypeStruct((b, t_len, ATTN_WIDTH), BF16),
        scratch_shapes=[pltpu.VMEM((tk, KV_WIDTH), BF16), pltpu.VMEM((tk, KV_WIDTH), BF16),
                        pltpu.VMEM((2, tk, V7X_LANES), BF16),
                        pltpu.VMEM((n_chunks, KEY_CHUNK, Q_BLOCK), jnp.int32),
                        pltpu.VMEM((V7X_SUBLANES_F32, Q_BLOCK), jnp.int32),
                        pltpu.VMEM((V7X_SUBLANES_F32, Q_BLOCK), F32),
                        pltpu.VMEM((n_chunks, Q_BLOCK, KEY_CHUNK), F32),
                        pltpu.VMEM((N_KV_HEADS, n_chunks, rows, KEY_CHUNK), F32),
                        pltpu.VMEM((N_KV_HEADS, rows, V7X_LANES), F32),
                        pltpu.VMEM((N_KV_HEADS, rows, V7X_LANES), F32),
                        pltpu.VMEM((N_KV_HEADS, rows, HEAD_DIM), F32)],
        compiler_params=_params("arbitrary", "arbitrary"),
        name="attn_prompt",
    )(q, iq, iwt, kb, vb, ik2)


def _pool_project(d_parts, pw_ref, ps_ref):
    group = d_parts[0].shape[1]
    outs = []
    for g, d in enumerate(d_parts):
        outs.append(_dot(d.astype(BF16), pw_ref[g]) * ps_ref[:, g * group:(g + 1) * group])
    return jnp.concatenate(outs, axis=1).astype(BF16)


def _pos_in_seq(i, tm, seq_len):
    t = (i * tm) % seq_len + lax.broadcasted_iota(jnp.int32, (tm, 1), 0)
    return jnp.where(t >= seq_len, t - seq_len, t)


def _mix_prompt_kernel(*refs, seq_len, per_seq):
    a_ref, u_ref, wo_ref, pw_ref, ps_ref, g_ref, gnext_ref, y_ref, yn_ref, e_scr = refs[-10:]
    x_refs = refs[:-10]
    i = pl.program_id(0)
    tm = u_ref.shape[0]
    group = pw_ref.shape[1]

    @pl.when(i == 0)
    def _():
        e_scr[0:POOL_HALO, :] = jnp.zeros((POOL_HALO, e_scr.shape[1]), F32)

    u = u_ref[...]
    e_scr[POOL_HALO:POOL_HALO + tm, :] = u
    t = _pos_in_seq(i, tm, seq_len)
    in_seq = [None] + [(t >= s).astype(F32) for s in range(1, max(POOL_WINDOWS))]
    d_parts = []
    for g, w in enumerate(POOL_WINDOWS):
        cols = slice(g * group, (g + 1) * group)
        ug = u[:, cols]
        wsum = ug
        for s in range(1, w):
            wsum = wsum + in_seq[s] * e_scr[POOL_HALO - s:POOL_HALO - s + tm, cols]
        cnt = jnp.minimum(t + 1, w).astype(F32)
        d_parts.append(wsum / cnt - ug)
    e_scr[0:POOL_HALO, :] = u[tm - POOL_HALO:, :]
    p = _pool_project(d_parts, pw_ref, ps_ref)
    o = _dot(a_ref[...], wo_ref[0:ATTN_WIDTH, :]) + _dot(p, wo_ref[ATTN_WIDTH:, :])
    y = _stream_tile(x_refs, per_seq) + _rms(o, g_ref[...])
    y_ref[...] = y
    yn_ref[...] = _rms(y, gnext_ref[...]).astype(BF16)


def _mix_prompt(x, a, u, wo, pw, ps, g, g_next, tm, seq_len, meta=None):
    x_specs, x_ops, per_seq = _stream_specs(x, tm, meta)
    d = x.shape[-1]
    m, pool_width = u.shape
    assert POOL_HALO <= tm <= seq_len
    row = lambda c: pl.BlockSpec((tm, c), lambda i: (i, 0))
    return pl.pallas_call(
        functools.partial(_mix_prompt_kernel, seq_len=seq_len, per_seq=per_seq),
        grid=(m // tm,),
        in_specs=x_specs + [row(ATTN_WIDTH), row(pool_width), _const_spec(wo.shape), _const_spec(pw.shape),
                            _const_spec((1, pool_width)), _const_spec((1, d)), _const_spec((1, d))],
        out_specs=[row(d), row(d)],
        out_shape=[jax.ShapeDtypeStruct((m, d), F32), jax.ShapeDtypeStruct((m, d), BF16)],
        scratch_shapes=[pltpu.VMEM((POOL_HALO + tm, pool_width), F32)],
        compiler_params=_params("arbitrary"),
        name="mix_prompt",
    )(*x_ops, a, u, wo, pw, ps, g, g_next)


def _ffn_up_kernel(xn_ref, wg_ref, wv_ref, cwg_ref, cwv_ref, cbg_ref, cbv_ref,
                   act_ref, lastg_ref, lastv_ref, hg_scr, hv_scr, carryg_scr, carryv_scr, *, seq_len):
    i = pl.program_id(1)
    tm = xn_ref.shape[0]

    @pl.when(i == 0)
    def _():
        carryg_scr[...] = jnp.zeros(carryg_scr.shape, F32)
        carryv_scr[...] = jnp.zeros(carryv_scr.shape, F32)

    t = _pos_in_seq(i, tm, seq_len)
    in_seq1 = (t >= 1).astype(F32)
    in_seq2 = (t >= 2).astype(F32)
    seq_end = jnp.minimum(seq_len - (i * tm) % seq_len, tm)
    last_row0 = pl.multiple_of(seq_end, CONV_HALO)

    xn = xn_ref[...]
    acts = []
    for w_ref, cw_ref, cb_ref, h_scr, carry_scr, last_ref in (
            (wg_ref, cwg_ref, cbg_ref, hg_scr, carryg_scr, lastg_ref),
            (wv_ref, cwv_ref, cbv_ref, hv_scr, carryv_scr, lastv_ref)):
        h = _dot(xn, w_ref[...])
        h_scr[0:CONV_HALO, :] = carry_scr[...]
        h_scr[CONV_HALO:CONV_HALO + tm, :] = h
        carry_scr[...] = h[tm - CONV_HALO:, :]
        last_ref[0] = h_scr[pl.ds(last_row0, CONV_HALO), :]
        acts.append(cb_ref[...]
                    + cw_ref[0:1, :] * (in_seq2 * h_scr[CONV_HALO - 2:CONV_HALO - 2 + tm, :])
                    + cw_ref[1:2, :] * (in_seq1 * h_scr[CONV_HALO - 1:CONV_HALO - 1 + tm, :])
                    + cw_ref[2:3, :] * h)
    cg, cv = acts
    act_ref[...] = (cg * jax.nn.sigmoid(cg) * cv).astype(BF16)


def _ffn_up(xn, wup, cw, cb, layer, tm, tf, seq_len):
    m, d = xn.shape
    d_ff = wup.shape[2] // 2
    nj = d_ff // tf
    n_tiles = m // tm
    assert CONV_HALO <= tm <= seq_len and seq_len % CONV_HALO == 0 and tm % CONV_HALO == 0
    return pl.pallas_call(
        functools.partial(_ffn_up_kernel, seq_len=seq_len),
        grid=(nj, n_tiles),
        in_specs=[
            pl.BlockSpec((tm, d), lambda j, i: (i, 0)),
            pl.BlockSpec((None, d, tf), lambda j, i: (layer, 0, j)),
            pl.BlockSpec((None, d, tf), lambda j, i: (layer, 0, j + nj)),
            pl.BlockSpec((CONV_WIDTH, tf), lambda j, i: (0, j)),
            pl.BlockSpec((CONV_WIDTH, tf), lambda j, i: (0, j + nj)),
            pl.BlockSpec((1, tf), lambda j, i: (0, j)),
            pl.BlockSpec((1, tf), lambda j, i: (0, j + nj)),
        ],
        out_specs=[
            pl.BlockSpec((tm, tf), lambda j, i: (i, j)),
            pl.BlockSpec((1, CONV_HALO, tf), lambda j, i: (i, 0, j)),
            pl.BlockSpec((1, CONV_HALO, tf), lambda j, i: (i, 0, j)),
        ],
        out_shape=[jax.ShapeDtypeStruct((m, d_ff), BF16),
                   jax.ShapeDtypeStruct((n_tiles, CONV_HALO, d_ff), F32),
                   jax.ShapeDtypeStruct((n_tiles, CONV_HALO, d_ff), F32)],
        scratch_shapes=[pltpu.VMEM((CONV_HALO + tm, tf), F32), pltpu.VMEM((CONV_HALO + tm, tf), F32),
                        pltpu.VMEM((CONV_HALO, tf), F32), pltpu.VMEM((CONV_HALO, tf), F32)],
        compiler_params=_params("arbitrary", "arbitrary"),
        name="ffn_up",
    )(xn, wup, wup, cw, cw, cb, cb)


def _ffn_down_kernel(act_ref, x_ref, wd_ref, g_ref, y_ref):
    y_ref[...] = x_ref[...] + _rms(_dot(act_ref[...], wd_ref[...]), g_ref[...])


def _ffn_down_tail(act, x, wd, g, layer, n_seq, n_skip, tile):
    m, d = x.shape
    d_ff = act.shape[1]
    seq_len = m // n_seq
    rows = seq_len - n_skip
    assert rows % tile == 0 and n_skip % V7X_SUBLANES_BF16 == 0
    seq_rows = lambda c: pl.BlockSpec((pl.Element(1), pl.Element(tile), pl.Element(c)),
                                      lambda s, k: (s, pl.multiple_of(n_skip + k * tile, V7X_SUBLANES_BF16), 0))

    def body(act_ref, x_ref, wd_ref, g_ref, y_ref):
        _ffn_down_kernel(act_ref.at[0], x_ref.at[0], wd_ref, g_ref, y_ref)

    return pl.pallas_call(
        body,
        grid=(n_seq, rows // tile),
        in_specs=[seq_rows(d_ff), seq_rows(d),
                  pl.BlockSpec((None, d_ff, d), lambda s, k: (layer, 0, 0), pipeline_mode=pl.Buffered(1)),
                  _const_spec((1, d))],
        out_specs=pl.BlockSpec((None, tile, d), lambda s, k: (s, k, 0)),
        out_shape=jax.ShapeDtypeStruct((n_seq, rows, d), F32),
        compiler_params=_params("arbitrary", "arbitrary"),
        name="ffn_down_tail",
    )(act.reshape(n_seq, seq_len, d_ff), x.reshape(n_seq, seq_len, d), wd, g)


def _ffn_down(act, x, wd, g, layer, tm):
    m, d = x.shape
    d_ff = act.shape[1]
    row = lambda c: pl.BlockSpec((tm, c), lambda i: (i, 0))
    return pl.pallas_call(
        _ffn_down_kernel,
        grid=(m // tm,),
        in_specs=[row(d_ff), row(d),
                  pl.BlockSpec((None, d_ff, d), lambda i: (layer, 0, 0), pipeline_mode=pl.Buffered(1)),
                  _const_spec((1, d))],
        out_specs=row(d),
        out_shape=jax.ShapeDtypeStruct((m, d), F32),
        compiler_params=_params("arbitrary"),
        name="ffn_down",
    )(act, x, wd, g)


def _idx_sample_kernel(pt_ref, iq_ref, iw_ref, new_ref, *refs):
    del pt_ref
    page_refs, o_ref, onew_ref = refs[:-2], refs[-2], refs[-1]
    iq = iq_ref[0]
    iw = iw_ref[0] * (IDX_DIM ** -0.5 * N_IDX_HEADS ** -0.5)

    def scores(keys_t):
        s = _dot(iq, keys_t.astype(BF16))
        return jnp.sum(jnp.maximum(s, 0.0) * iw, axis=0, keepdims=True)

    o_ref[0] = scores(jnp.concatenate([r[0, 0] for r in page_refs], axis=1))
    onew_ref[0] = scores(new_ref[0])


def _page_specs(layer, block, n_pages, per_step):
    def spec(p):
        def index(b, c, pt):
            return (layer, pt[b * n_pages + c * per_step + p]) + (0,) * (len(block) - 2)
        return pl.BlockSpec(block, index)
    return [spec(p) for p in range(per_step)]


def _idx_sample(pt_flat, iq3, iw3, new_page, cache_kidx, layer, n_pages):
    db = iq3.shape[0]
    per_step = min(n_pages, IDX_PAGES_PER_STEP)
    assert n_pages % per_step == 0
    grid_spec = pltpu.PrefetchScalarGridSpec(
        num_scalar_prefetch=1,
        grid=(db, n_pages // per_step),
        in_specs=[pl.BlockSpec((1, N_IDX_HEADS, IDX_DIM), lambda b, c, pt: (b, 0, 0)),
                  pl.BlockSpec((1, N_IDX_HEADS, 1), lambda b, c, pt: (b, 0, 0)),
                  pl.BlockSpec((1, IDX_DIM, PAGE_SIZE), lambda b, c, pt: (b, 0, 0))]
                 + _page_specs(layer, (1, 1, IDX_DIM, PAGE_SIZE), n_pages, per_step),
        out_specs=[pl.BlockSpec((1, 1, per_step * PAGE_SIZE), lambda b, c, pt: (b, 0, c)),
                   pl.BlockSpec((1, 1, PAGE_SIZE), lambda b, c, pt: (b, 0, 0))],
    )
    return pl.pallas_call(
        _idx_sample_kernel,
        grid_spec=grid_spec,
        out_shape=[jax.ShapeDtypeStruct((db, 1, n_pages * PAGE_SIZE), F32),
                   jax.ShapeDtypeStruct((db, 1, PAGE_SIZE), F32)],
        compiler_params=_params("arbitrary", "arbitrary"),
        name="idx_sample",
    )(pt_flat, iq3, iw3, new_page, *([cache_kidx] * per_step))


def _select_sample_kernel(sc_ref, scn_ref, bias_ref, biasn_ref, key_scr, *, topk):
    db = sc_ref.shape[0]
    key_new = _sort_key(scn_ref[:, 0:1])
    key_scr[...] = _sort_key(sc_ref[...])

    def count_ge(t):
        past = jnp.sum(jnp.where(key_scr[...] >= t, 1.0, 0.0), axis=1, keepdims=True)
        return past + jnp.where(key_new >= t, 1.0, 0.0)

    thr, cnt_t = _kth_largest_key(count_ge, (db, 1), topk)
    tied = jnp.max(jnp.where(cnt_t > topk, 1.0, 0.0)) > 0.0

    @pl.when(jnp.logical_not(tied))
    def _():
        bias_ref[...] = jnp.where(key_scr[...] >= thr, 0.0, -jnp.inf)
        biasn_ref[...] = jnp.where(key_new >= thr, 0.0, -jnp.inf)

    @pl.when(tied)
    def _():
        above = (jnp.sum(jnp.where(key_scr[...] > thr, 1.0, 0.0), axis=1, keepdims=True)
                 + jnp.where(key_new > thr, 1.0, 0.0))
        room = topk - above
        rank = _tie_rank_matrix(V7X_LANES, lower=False)
        seen = jnp.zeros((db, 1), F32)
        for blk in range(sc_ref.shape[1] // V7X_LANES):
            cols = slice(blk * V7X_LANES, (blk + 1) * V7X_LANES)
            key = key_scr[:, cols]
            eq = jnp.where(key == thr, 1.0, 0.0)
            pos = _dot(eq.astype(BF16), rank) + seen
            keep = jnp.where(key > thr, 1.0, jnp.where(pos <= room, eq, 0.0))
            bias_ref[:, cols] = jnp.where(keep > 0.0, 0.0, -jnp.inf)
            seen = seen + jnp.sum(eq, axis=1, keepdims=True)
        keep_new = (key_new > thr) | ((key_new == thr) & (seen + 1.0 <= room))
        biasn_ref[...] = jnp.where(keep_new, 0.0, -jnp.inf)


def _select_sample(scores, scores_new, topk):
    db, past = scores.shape
    return pl.pallas_call(
        functools.partial(_select_sample_kernel, topk=topk),
        out_shape=[jax.ShapeDtypeStruct((db, past), F32), jax.ShapeDtypeStruct((db, 1), F32)],
        scratch_shapes=[pltpu.VMEM((db, past), jnp.int32)],
        compiler_params=pltpu.CompilerParams(vmem_limit_bytes=V7X_VMEM_LIMIT_BYTES),
        name="select_sample",
    )(scores, scores_new)


def _attn_sample_kernel(pt_ref, q_ref, bias_ref, biasn_ref, kn_ref, vn_ref, *refs):
    del pt_ref
    per_step = (len(refs) - 4) // 2
    k_refs = refs[:per_step]
    v_refs = refs[per_step:2 * per_step]
    o_ref, m_scr, l_scr, acc_scr = refs[2 * per_step:]
    c = pl.program_id(1)
    scale = HEAD_DIM ** -0.5

    @pl.when(c == 0)
    def _():
        m_scr[...] = jnp.full(m_scr.shape, -jnp.inf, F32)
        l_scr[...] = jnp.zeros(l_scr.shape, F32)
        acc_scr[...] = jnp.zeros(acc_scr.shape, F32)

    def update(g, s, v):
        m_old = m_scr[g]
        m_new = jnp.maximum(m_old, jnp.max(s, axis=-1, keepdims=True))
        safe = jnp.where(m_new == -jnp.inf, 0.0, m_new)
        alpha = jnp.exp(m_old - safe)
        p = jnp.exp(s - safe)
        l_scr[g] = alpha * l_scr[g] + jnp.sum(p, axis=-1, keepdims=True)
        acc_scr[g] = alpha * acc_scr[g] + _dot(p.astype(BF16), v)
        m_scr[g] = m_new

    bias = bias_ref[0]
    for g in range(N_KV_HEADS):
        kg = jnp.concatenate([r[0, 0, pl.ds(g, PAGE_SIZE, stride=N_KV_HEADS), :] for r in k_refs], axis=0)
        vg = jnp.concatenate([r[0, 0, pl.ds(g, PAGE_SIZE, stride=N_KV_HEADS), :] for r in v_refs], axis=0)
        s = _dot_t(q_ref[0, g], kg.astype(BF16)) * scale + bias
        update(g, s, vg.astype(BF16))

    @pl.when(c == pl.num_programs(1) - 1)
    def _():
        for g in range(N_KV_HEADS):
            kn = kn_ref[0, :, g * HEAD_DIM:(g + 1) * HEAD_DIM].astype(BF16)
            vn = vn_ref[0, :, g * HEAD_DIM:(g + 1) * HEAD_DIM].astype(BF16)
            qf = q_ref[0, g].astype(F32)
            s = jnp.sum(qf * kn.astype(F32), axis=-1, keepdims=True) * scale + biasn_ref[0]
            m_old = m_scr[g]
            m_new = jnp.maximum(m_old, s)
            alpha = jnp.exp(m_old - m_new)
            p = jnp.exp(s - m_new)
            l = alpha * l_scr[g] + p
            acc = alpha * acc_scr[g] + p.astype(BF16).astype(F32) * vn.astype(F32)
            o_ref[0, g] = acc / l


def _attn_sample(pt_flat, q4, bias3, bias_new, k_new, v_new, cache_k, cache_v, layer, n_pages):
    db, _, rows, _ = q4.shape
    per_step = min(n_pages, ATTN_PAGES_PER_STEP)
    assert n_pages % per_step == 0
    page_block = (1, 1, PAGE_SIZE * N_KV_HEADS, HEAD_DIM)
    pages = _page_specs(layer, page_block, n_pages, per_step)
    grid_spec = pltpu.PrefetchScalarGridSpec(
        num_scalar_prefetch=1,
        grid=(db, n_pages // per_step),
        in_specs=[pl.BlockSpec((1, N_KV_HEADS, rows, HEAD_DIM), lambda b, c, pt: (b, 0, 0, 0)),
                  pl.BlockSpec((1, 1, per_step * PAGE_SIZE), lambda b, c, pt: (b, 0, c)),
                  pl.BlockSpec((1, 1, 1), lambda b, c, pt: (b, 0, 0)),
                  pl.BlockSpec((1, 1, KV_WIDTH), lambda b, c, pt: (b, 0, 0)),
                  pl.BlockSpec((1, 1, KV_WIDTH), lambda b, c, pt: (b, 0, 0))] + pages + pages,
        out_specs=pl.BlockSpec((1, N_KV_HEADS, rows, HEAD_DIM), lambda b, c, pt: (b, 0, 0, 0)),
        scratch_shapes=[pltpu.VMEM((N_KV_HEADS, rows, 1), F32),
                        pltpu.VMEM((N_KV_HEADS, rows, 1), F32),
                        pltpu.VMEM((N_KV_HEADS, rows, HEAD_DIM), F32)],
    )
    return pl.pallas_call(
        _attn_sample_kernel,
        grid_spec=grid_spec,
        out_shape=jax.ShapeDtypeStruct((db, N_KV_HEADS, rows, HEAD_DIM), F32),
        compiler_params=_params("arbitrary", "arbitrary"),
        name="attn_sample",
    )(pt_flat, q4, bias3, bias_new, k_new, v_new, *([cache_k] * per_step), *([cache_v] * per_step))


def _mix_sample_kernel(x_ref, a_ref, e_ref, wo_ref, pw_ref, ps_ref, g_ref, y_ref):
    group = pw_ref.shape[1]
    n_rows = e_ref.shape[0]
    d_parts = []
    for g, w in enumerate(POOL_WINDOWS):
        cols = slice(g * group, (g + 1) * group)
        ug = e_ref[n_rows - 1, :, cols]
        wsum = ug
        for s in range(1, w):
            wsum = wsum + e_ref[n_rows - 1 - s, :, cols]
        d_parts.append(wsum / float(w) - ug)
    p = _pool_project(d_parts, pw_ref, ps_ref)
    o = _dot(a_ref[...].astype(BF16), wo_ref[0:ATTN_WIDTH, :]) + _dot(p, wo_ref[ATTN_WIDTH:, :])
    y_ref[...] = x_ref[...] + _rms(o, g_ref[...])


def _mix_sample(x, a, e, wo, pw, ps, g):
    return pl.pallas_call(
        _mix_sample_kernel,
        out_shape=jax.ShapeDtypeStruct(x.shape, F32),
        compiler_params=pltpu.CompilerParams(vmem_limit_bytes=V7X_VMEM_LIMIT_BYTES),
        name="mix_sample",
    )(x, a, e, wo, pw, ps, g)


def _ffn_sample_kernel(x_ref, gpre_ref, wg_ref, wv_ref, cwg_ref, cwv_ref, cbg_ref, cbv_ref, sg_ref, sv_ref,
                       wd_ref, gpost_ref, y_ref, hg_ref, hv_ref, xn_scr, acc_scr):
    j = pl.program_id(0)

    @pl.when(j == 0)
    def _():
        xn_scr[...] = _rms(x_ref[...], gpre_ref[...]).astype(BF16)
        acc_scr[...] = jnp.zeros(acc_scr.shape, F32)

    xn = xn_scr[...]
    hg = _dot(xn, wg_ref[...])
    hv = _dot(xn, wv_ref[...])
    hg_ref[...] = hg
    hv_ref[...] = hv

    def conv(h, s_ref, cw_ref, cb_ref):
        return cb_ref[...] + cw_ref[0:1, :] * s_ref[0] + cw_ref[1:2, :] * s_ref[1] + cw_ref[2:3, :] * h

    cg = conv(hg, sg_ref, cwg_ref, cbg_ref)
    cv = conv(hv, sv_ref, cwv_ref, cbv_ref)
    act = (cg * jax.nn.sigmoid(cg) * cv).astype(BF16)
    acc_scr[...] += _dot(act, wd_ref[...])

    @pl.when(j == pl.num_programs(0) - 1)
    def _():
        y_ref[...] = x_ref[...] + _rms(acc_scr[...], gpost_ref[...])


def _ffn_sample(x, gpre, wup, cw, cb, state, wd, gpost, layer, tf):
    db, d = x.shape
    d_ff = wd.shape[1]
    nj = d_ff // tf
    n_state = state.shape[0]
    return pl.pallas_call(
        _ffn_sample_kernel,
        grid=(nj,),
        in_specs=[
            pl.BlockSpec((db, d), lambda j: (0, 0)),
            pl.BlockSpec((1, d), lambda j: (0, 0)),
            pl.BlockSpec((None, d, tf), lambda j: (layer, 0, j)),
            pl.BlockSpec((None, d, tf), lambda j: (layer, 0, j + nj)),
            pl.BlockSpec((CONV_WIDTH, tf), lambda j: (0, j)),
            pl.BlockSpec((CONV_WIDTH, tf), lambda j: (0, j + nj)),
            pl.BlockSpec((1, tf), lambda j: (0, j)),
            pl.BlockSpec((1, tf), lambda j: (0, j + nj)),
            pl.BlockSpec((n_state, db, tf), lambda j: (0, 0, j)),
            pl.BlockSpec((n_state, db, tf), lambda j: (0, 0, j + nj)),
            pl.BlockSpec((None, tf, d), lambda j: (layer, j, 0)),
            pl.BlockSpec((1, d), lambda j: (0, 0)),
        ],
        out_specs=[pl.BlockSpec((db, d), lambda j: (0, 0)),
                   pl.BlockSpec((db, tf), lambda j: (0, j)),
                   pl.BlockSpec((db, tf), lambda j: (0, j))],
        out_shape=[jax.ShapeDtypeStruct((db, d), F32), jax.ShapeDtypeStruct((db, d_ff), F32),
                   jax.ShapeDtypeStruct((db, d_ff), F32)],
        scratch_shapes=[pltpu.VMEM((db, d), BF16), pltpu.VMEM((db, d), F32)],
        compiler_params=_params("arbitrary"),
        name="ffn_sample",
    )(x, gpre, wup, wup, cw, cw, cb, cb, state, state, wd, gpost)


def _split_w_in(w_in_l):
    d = w_in_l.shape[0]
    q_end = ATTN_WIDTH
    k_end = q_end + KV_WIDTH
    v_end = k_end + KV_WIDTH
    iq_end = v_end + N_IDX_HEADS * IDX_DIM
    ik_end = iq_end + IDX_DIM
    iw_end = ik_end + N_IDX_HEADS
    w = w_in_l.astype(BF16)
    w_ik = w[:, iq_end:ik_end]
    zeros = lambda c: jnp.zeros((d, c), BF16)
    idx = jnp.concatenate([w_ik, zeros(V7X_LANES - IDX_DIM), zeros(V7X_LANES - IDX_DIM), w_ik,
                           w[:, ik_end:iw_end], zeros(V7X_LANES - N_IDX_HEADS)], axis=1)
    return {"q": w[:, :q_end], "kv": w[:, q_end:v_end], "iq": w[:, v_end:iq_end], "idx": idx, "u": w[:, iw_end:]}


def kernel(x_prompt, x_sample, cache_k, cache_v, cache_kidx, state_pool, state_conv, page_table, meta_tokens,
           w_in, w_o, pool_w, pool_scale, g_mix_pre, g_mix_post, g_ffn_pre, g_ffn_post, w_up, conv_w, conv_b,
           w_down):
    b, seq, d = x_prompt.shape
    db, ds, _ = x_sample.shape
    assert ds == 1, "the sample kernels handle one new token per sequence"
    depth = w_in.shape[0]
    n_pool = cache_k.shape[1]
    n_pages = page_table.shape[1]
    assert cache_k.shape[2] == PAGE_SIZE
    past = n_pages * PAGE_SIZE
    t_len = seq + N_META
    d_ff = w_down.shape[1]
    pool_width = d - ATTN_WIDTH
    topk_p = min(TOPK_MAX, t_len // 4)
    topk_s = min(TOPK_MAX, (past + ds) // 4)
    assert past >= max(POOL_WINDOWS)

    m_rows = b * t_len
    tm_proj = _row_tile(m_rows, min(TM_PROJ_CAP, t_len))
    tm_ffn = _row_tile(m_rows, min(TM_FFN_CAP, t_len))
    tm_down = _row_tile(m_rows, min(TM_DOWN_CAP, t_len))
    tf = _ff_tile(d_ff)
    sub8 = V7X_SUBLANES_F32
    heads_pad = -(-HEADS_PER_KV // sub8) * sub8

    tm_first = _row_tile(t_len, TM_FFN_CAP)
    meta = meta_tokens.astype(x_prompt.dtype)
    first_in_place = tm_first > N_META
    if first_in_place:
        xp = x_prompt
    else:
        xp = jnp.concatenate([jnp.broadcast_to(meta[None], (b, N_META, d)), x_prompt], axis=1).reshape(m_rows, d)
    xs = x_sample.reshape(db, d)
    pt_flat = page_table.reshape(-1).astype(jnp.int32)
    ck = cache_k.reshape(depth, n_pool, PAGE_SIZE * N_KV_HEADS, HEAD_DIM)
    cv = cache_v.reshape(depth, n_pool, PAGE_SIZE * N_KV_HEADS, HEAD_DIM)
    kidx_t = jnp.swapaxes(cache_kidx, 2, 3)

    row2 = lambda a: a.reshape(1, -1)
    outs = {name: [] for name in ("kp", "vp", "ikp", "poolp", "convp", "ks", "vs", "iks", "pools", "convs")}
    wup = w_up.astype(BF16)
    wd = w_down.astype(BF16)
    for l in range(depth):
        w = _split_w_in(w_in[l])
        wo = w_o[l].astype(BF16)
        pw = pool_w[l].astype(BF16)
        ps = row2(pool_scale[l])
        cw = conv_w[l]
        cb = row2(conv_b[l])

        stream = dict(tm=tm_first, meta=meta) if (l == 0 and first_in_place) else dict(tm=tm_proj)
        q, k, v, kb, vb, iq, ik, ik2, iw, u = _in_proj(xp, row2(g_mix_pre[l]), w, **stream)
        b3 = lambda a: a.reshape(b, t_len, a.shape[-1])
        a = _attn_prompt(b3(q), b3(iq), jnp.swapaxes(b3(iw), 1, 2), b3(kb), b3(vb), b3(ik2), topk_p)
        xp, xn = _mix_prompt(xp, a.reshape(m_rows, ATTN_WIDTH), u, wo, pw, ps, row2(g_mix_post[l]),
                             row2(g_ffn_pre[l]), seq_len=t_len, **stream)
        act, last_g, last_v = _ffn_up(xn, wup, cw, cb, l, tm_ffn, tf, t_len)
        if l < depth - 1:
            xp = _ffn_down(act, xp, wd, row2(g_ffn_post[l]), l, tm_down)
        else:
            y_prompt = _ffn_down_tail(act, xp, wd, row2(g_ffn_post[l]), l, b, N_META, _row_tile(seq, TM_TAIL_CAP))
        outs["kp"].append(k.reshape(b, t_len, N_KV_HEADS, HEAD_DIM))
        outs["vp"].append(v.reshape(b, t_len, N_KV_HEADS, HEAD_DIM))
        outs["ikp"].append(ik.reshape(b, t_len, IDX_DIM))
        outs["poolp"].append(u.reshape(b, t_len, pool_width)[:, t_len - POOL_BUF:])
        tail = slice(CONV_HALO - (CONV_WIDTH - 1), CONV_HALO)
        if t_len % tm_ffn == 0:
            per_seq = t_len // tm_ffn
            seq_last = lambda h: h[per_seq - 1::per_seq, tail]
        else:
            end_tiles = [((s + 1) * t_len - 1) // tm_ffn for s in range(b)]
            seq_last = lambda h: jnp.stack([h[i, tail] for i in end_tiles])
        outs["convp"].append(jnp.concatenate([seq_last(last_g), seq_last(last_v)], axis=-1))

        q, k, v, _, _, iq, ik, _, iw, u = _in_proj(xs, row2(g_mix_pre[l]), w, db)
        iq3 = iq.reshape(db, N_IDX_HEADS, IDX_DIM)
        new_page = jnp.pad(ik[:, :, None], ((0, 0), (0, 0), (0, PAGE_SIZE - 1)))
        scores, scores_new = _idx_sample(pt_flat, iq3, iw.reshape(db, N_IDX_HEADS, 1), new_page, kidx_t, l, n_pages)
        bias, bias_new = _select_sample(scores.reshape(db, past), scores_new.reshape(db, PAGE_SIZE), topk_s)
        q4 = jnp.pad(q.reshape(db, N_KV_HEADS, HEADS_PER_KV, HEAD_DIM),
                     ((0, 0), (0, 0), (0, heads_pad - HEADS_PER_KV), (0, 0)))
        a = _attn_sample(pt_flat, q4, bias.reshape(db, 1, past), bias_new.reshape(db, 1, 1),
                         k.reshape(db, 1, KV_WIDTH), v.reshape(db, 1, KV_WIDTH), ck, cv, l, n_pages)
        a = a[:, :, :HEADS_PER_KV].reshape(db, ATTN_WIDTH)
        e = jnp.concatenate([jnp.swapaxes(state_pool[l], 0, 1).astype(u.dtype), u[None]], axis=0)
        xs = _mix_sample(xs, a, e, wo, pw, ps, row2(g_mix_post[l]))
        xs, hg, hv = _ffn_sample(xs, row2(g_ffn_pre[l]), wup, cw, cb, jnp.swapaxes(state_conv[l], 0, 1), wd,
                                 row2(g_ffn_post[l]), l, tf)
        outs["ks"].append(k.reshape(db, ds, N_KV_HEADS, HEAD_DIM))
        outs["vs"].append(v.reshape(db, ds, N_KV_HEADS, HEAD_DIM))
        outs["iks"].append(ik.reshape(db, ds, IDX_DIM))
        outs["pools"].append(jnp.concatenate([state_pool[l].astype(u.dtype), u[:, None, :]], axis=1)[:, 1:])
        h = jnp.concatenate([hg, hv], axis=-1)[:, None, :]
        outs["convs"].append(jnp.concatenate([state_conv[l].astype(h.dtype), h], axis=1)[:, -(CONV_WIDTH - 1):])

    y_sample = xs.reshape(db, ds, d)
    stack = lambda name: jnp.stack(outs[name])
    return (y_prompt, y_sample, stack("kp"), stack("vp"), stack("ikp"), stack("poolp"), stack("convp"),
            stack("ks"), stack("vs"), stack("iks"), stack("pools"), stack("convs"))
```

```python
import functools

import jax
import jax.numpy as jnp
from jax import lax
from jax.experimental import pallas as pl
from jax.experimental.pallas import tpu as pltpu

N_META = 16
N_HEADS = 8
N_KV_HEADS = 2
HEAD_DIM = 128
N_IDX_HEADS = 16
IDX_DIM = 64
TOPK_MAX = 256
POOL_WINDOWS = (2, 4, 8, 16)
CONV_WIDTH = 3
PAGE_SIZE = 128
Q_BLOCK = 256
RMS_EPS = 1e-6

ATTN_WIDTH = N_HEADS * HEAD_DIM
KV_WIDTH = N_KV_HEADS * HEAD_DIM
HEADS_PER_KV = N_HEADS // N_KV_HEADS
POOL_BUF = max(POOL_WINDOWS) - 1

V7X_LANES = 128
V7X_SUBLANES_F32 = 8
V7X_SUBLANES_BF16 = 16
V7X_VMEM_LIMIT_BYTES = 56 * 1024 * 1024

INT_MIN = -(2 ** 31)
F32 = jnp.float32
BF16 = jnp.bfloat16

CONV_HALO = V7X_SUBLANES_F32
POOL_HALO = 2 * V7X_SUBLANES_F32
IDX_PAGES_PER_STEP = 64
ATTN_PAGES_PER_STEP = 64
KEY_CHUNK = 256
TM_PROJ_CAP = 384
TM_DOWN_CAP = 384
TM_TAIL_CAP = 256
TM_FFN_CAP = 688


def _params(*sem):
    return pltpu.CompilerParams(dimension_semantics=sem, vmem_limit_bytes=V7X_VMEM_LIMIT_BYTES)


def _row_tile(seq_len, cap):
    best = None
    for d in range(V7X_SUBLANES_BF16, min(seq_len, cap) + 1, V7X_SUBLANES_BF16):
        if seq_len % d == 0:
            best = d
    assert best is not None, seq_len
    return best


def _ff_tile(d_ff, cap=512):
    best = None
    for d in range(V7X_LANES, min(d_ff, cap) + 1, V7X_LANES):
        if d_ff % d == 0:
            best = d
    assert best is not None, d_ff
    return best


def _rms(x, g):
    return x * lax.rsqrt(jnp.mean(x * x, axis=-1, keepdims=True) + RMS_EPS) * g


def _dot(a, b):
    return jnp.dot(a, b, preferred_element_type=F32)


def _dot_t(a, b):
    return lax.dot_general(a, b, (((1,), (1,)), ((), ())), preferred_element_type=F32)


def _const_spec(shape):
    n = len(shape)
    return pl.BlockSpec(shape, lambda *_: (0,) * n, pipeline_mode=pl.Buffered(1))


def _stream_specs(x, tm, meta):
    d = x.shape[-1]
    if meta is None:
        return [pl.BlockSpec((tm, d), lambda i: (i, 0))], [x], 0
    n_meta = meta.shape[0]
    per_seq = (x.shape[1] + n_meta) // tm
    assert per_seq * tm == x.shape[1] + n_meta and n_meta % V7X_SUBLANES_BF16 == 0 and tm > n_meta

    def window(i):
        start = jnp.maximum((i % per_seq) * tm - n_meta, 0)
        return i // per_seq, pl.multiple_of(start, V7X_SUBLANES_BF16), 0

    return ([pl.BlockSpec((pl.Element(1), pl.Element(tm), pl.Element(d)), window), _const_spec(meta.shape)],
            [x, meta], per_seq)


def _stream_tile(x_refs, per_seq):
    if len(x_refs) == 1:
        return x_refs[0][...]
    win_ref, meta_ref = x_refs
    win = win_ref[0]
    n_meta = meta_ref.shape[0]
    first = jnp.concatenate([meta_ref[...], win[:win.shape[0] - n_meta]], axis=0)
    return jnp.where(pl.program_id(0) % per_seq == 0, first, win)


def _in_proj_kernel(*refs, per_seq):
    (g_ref, wq_ref, wkv_ref, wiq_ref, widx_ref, wu_ref,
     q_ref, k_ref, v_ref, kb_ref, vb_ref, iq_ref, ik_ref, ik2_ref, iw_ref, u_ref) = refs[-16:]
    x = _stream_tile(refs[:-16], per_seq)
    hn = _rms(x, g_ref[...]).astype(BF16)
    q_ref[...] = _dot(hn, wq_ref[...]).astype(BF16)
    kv = _dot(hn, wkv_ref[...])
    tm = x.shape[0]
    for out_ref, c0 in ((k_ref, 0), (v_ref, KV_WIDTH)):
        for g in range(N_KV_HEADS):
            out_ref[pl.ds(g, tm, stride=N_KV_HEADS), :] = kv[:, c0 + g * HEAD_DIM:c0 + (g + 1) * HEAD_DIM]
    kb_ref[...] = kv[:, :KV_WIDTH].astype(BF16)
    vb_ref[...] = kv[:, KV_WIDTH:].astype(BF16)
    iq_ref[...] = _dot(hn, wiq_ref[...]).astype(BF16)
    idx = _dot(hn, widx_ref[...])
    ik_ref[...] = idx[:, :IDX_DIM]
    ik2_ref[...] = idx[:, :2 * V7X_LANES].astype(BF16)
    iw_ref[...] = idx[:, 2 * V7X_LANES:2 * V7X_LANES + N_IDX_HEADS]
    u_ref[...] = _dot(hn, wu_ref[...])


def _in_proj(x, g, w, tm, meta=None):
    x_specs, x_ops, per_seq = _stream_specs(x, tm, meta)
    d = x.shape[-1]
    m = x.shape[0] if meta is None else x.shape[0] * (x.shape[1] + meta.shape[0])
    pool_width = w["u"].shape[1]
    outs = [
        (1, ATTN_WIDTH, BF16), (N_KV_HEADS, HEAD_DIM, F32), (N_KV_HEADS, HEAD_DIM, F32), (1, KV_WIDTH, BF16),
        (1, KV_WIDTH, BF16), (1, N_IDX_HEADS * IDX_DIM, BF16), (1, IDX_DIM, F32), (1, 2 * V7X_LANES, BF16),
        (1, N_IDX_HEADS, F32), (1, pool_width, F32),
    ]
    return pl.pallas_call(
        functools.partial(_in_proj_kernel, per_seq=per_seq),
        grid=(m // tm,),
        in_specs=x_specs + [_const_spec((1, d)), _const_spec(w["q"].shape), _const_spec(w["kv"].shape),
                            _const_spec(w["iq"].shape), _const_spec(w["idx"].shape), _const_spec(w["u"].shape)],
        out_specs=[pl.BlockSpec((r * tm, c), lambda i: (i, 0)) for r, c, _ in outs],
        out_shape=[jax.ShapeDtypeStruct((r * m, c), dt) for r, c, dt in outs],
        compiler_params=_params("arbitrary"),
        name="in_proj",
    )(*x_ops, g, w["q"], w["kv"], w["iq"], w["idx"], w["u"])


def _sort_key(x):
    bits = lax.bitcast_convert_type(x, jnp.int32)
    return bits ^ (lax.shift_right_arithmetic(bits, 31) & jnp.int32(0x7FFFFFFF))


def _kth_largest_key(count_ge, shape, k):
    def body(it, state):
        t, cnt_t = state
        cand = t | lax.shift_left(jnp.int32(1), 31 - it)
        cnt = count_ge(cand ^ jnp.int32(INT_MIN))
        take = cnt >= k
        return jnp.where(take, cand, t), jnp.where(take, cnt, cnt_t)

    t, cnt_t = lax.fori_loop(0, 32, body, (jnp.zeros(shape, jnp.int32), jnp.zeros(shape, F32)))
    return t ^ jnp.int32(INT_MIN), cnt_t


def _tie_rank_matrix(n, lower):
    r = lax.broadcasted_iota(jnp.int32, (n, n), 0)
    c = lax.broadcasted_iota(jnp.int32, (n, n), 1)
    return jnp.where(r >= c if lower else r <= c, 1.0, 0.0).astype(BF16)


def _attn_prompt_kernel(q_ref, iq_ref, iwt_ref, kb_ref, vb_ref, ik2_ref, o_ref,
                        k_scr, v_scr, ik_scr, key_scr, thr_scr, cnt_scr, bias_scr, s_scr, m_scr, l_scr, acc_scr,
                        *, topk):
    i = pl.program_id(1)
    t_len = kb_ref.shape[1]
    n_chunks, ch, _ = key_scr.shape
    tk = n_chunks * ch
    halves = ch // V7X_LANES
    rows = HEADS_PER_KV * Q_BLOCK
    nc = ((i + 1) * Q_BLOCK + ch - 1) // ch

    @pl.when(i == 0)
    def _():
        k_scr[0:t_len, :] = kb_ref[0]
        v_scr[0:t_len, :] = vb_ref[0]
        ik_scr[0, 0:t_len, :] = ik2_ref[0, :, 0:V7X_LANES]
        ik_scr[1, 0:t_len, :] = ik2_ref[0, :, V7X_LANES:]
        if tk > t_len:
            k_scr[t_len:tk, :] = jnp.zeros((tk - t_len, k_scr.shape[1]), BF16)
            v_scr[t_len:tk, :] = jnp.zeros((tk - t_len, v_scr.shape[1]), BF16)
            ik_scr[0, t_len:tk, :] = jnp.zeros((tk - t_len, V7X_LANES), BF16)
            ik_scr[1, t_len:tk, :] = jnp.zeros((tk - t_len, V7X_LANES), BF16)

    iw = iwt_ref[0] * (IDX_DIM ** -0.5 * N_IDX_HEADS ** -0.5)
    iq2 = [jnp.concatenate([iq_ref[0, :, (2 * jj + e) * V7X_LANES:(2 * jj + e + 1) * V7X_LANES] for e in range(2)],
                           axis=0) for jj in range(N_IDX_HEADS // 4)]
    qpos = i * Q_BLOCK + lax.broadcasted_iota(jnp.int32, (1, Q_BLOCK), 1)

    def over_chunks(chunk_fn):
        def pair(p, carry):
            chunk_fn(2 * p)
            chunk_fn(2 * p + 1)
            return carry

        lax.fori_loop(0, lax.shift_right_logical(nc, 1), pair, 0)

        @pl.when((nc & 1) == 1)
        def _():
            chunk_fn(nc - 1)

    def index_chunk(c):
        r0 = pl.multiple_of(c * ch, ch)
        keys = jnp.concatenate([ik_scr[0, pl.ds(r0, ch), :], ik_scr[1, pl.ds(r0, ch), :]], axis=0)
        acc = jnp.zeros((ch, Q_BLOCK), F32)
        for jj in range(N_IDX_HEADS // 4):
            s = _dot_t(keys, iq2[jj])
            for e in range(2):
                for half in range(2):
                    h = 2 * (2 * jj + e) + half
                    blk = s[half * ch:(half + 1) * ch, e * Q_BLOCK:(e + 1) * Q_BLOCK]
                    acc = acc + jnp.maximum(blk, 0.0) * iw[h:h + 1, :]
        kpos = r0 + lax.broadcasted_iota(jnp.int32, (ch, 1), 0)
        key_scr[c] = jnp.where(kpos <= qpos, _sort_key(acc), jnp.int32(INT_MIN))

    over_chunks(index_chunk)

    for n_static in range(1, n_chunks + 1):
        @pl.when(nc == n_static)
        def _(n_static=n_static):
            def count_ge(t):
                part = jnp.zeros((V7X_SUBLANES_F32, Q_BLOCK), F32)
                for c in range(n_static):
                    hit = jnp.where(key_scr[c] >= t, 1.0, 0.0)
                    part = part + jnp.sum(hit.reshape(ch // V7X_SUBLANES_F32, V7X_SUBLANES_F32, Q_BLOCK), axis=0)
                return jnp.sum(part, axis=0, keepdims=True)

            t, cnt_t = _kth_largest_key(count_ge, (1, Q_BLOCK), topk)
            thr_scr[...] = jnp.broadcast_to(jnp.maximum(t, jnp.int32(INT_MIN + 1)), thr_scr.shape)
            cnt_scr[...] = jnp.broadcast_to(cnt_t, cnt_scr.shape)

    thr = thr_scr[0:1, :]
    tied = jnp.max(jnp.where(cnt_scr[0:1, :] > topk, 1.0, 0.0)) > 0.0

    def store_bias(c, bias_t):
        for r in range(halves):
            bias_scr[c, :, r * V7X_LANES:(r + 1) * V7X_LANES] = bias_t[r * V7X_LANES:(r + 1) * V7X_LANES, :].T

    @pl.when(jnp.logical_not(tied))
    def _():
        over_chunks(lambda c: store_bias(c, jnp.where(key_scr[c] >= thr, 0.0, -jnp.inf)))

    @pl.when(tied)
    def _():
        def count_above(c, acc):
            return acc + jnp.sum(jnp.where(key_scr[c] > thr, 1.0, 0.0), axis=0, keepdims=True)

        room = topk - lax.fori_loop(0, nc, count_above, jnp.zeros((1, Q_BLOCK), F32))
        rank = _tie_rank_matrix(ch, lower=True)

        def tie_chunk(c, seen):
            key = key_scr[c]
            eq = jnp.where(key == thr, 1.0, 0.0)
            pos = _dot(rank, eq.astype(BF16)) + seen
            keep = jnp.where(key > thr, 1.0, jnp.where(pos <= room, eq, 0.0))
            store_bias(c, jnp.where(keep > 0.0, 0.0, -jnp.inf))
            return seen + jnp.sum(eq, axis=0, keepdims=True)

        lax.fori_loop(0, nc, tie_chunk, jnp.zeros((1, Q_BLOCK), F32))

    scale = HEAD_DIM ** -0.5
    gcols = [slice(g * HEAD_DIM, (g + 1) * HEAD_DIM) for g in range(N_KV_HEADS)]
    qgs = [jnp.concatenate([q_ref[0, :, (g * HEADS_PER_KV + r) * HEAD_DIM:(g * HEADS_PER_KV + r + 1) * HEAD_DIM]
                            for r in range(HEADS_PER_KV)], axis=0) for g in range(N_KV_HEADS)]
    lane_tiles = lambda x: [x[:, r * V7X_LANES:(r + 1) * V7X_LANES] for r in range(halves)]

    m_scr[...] = jnp.full(m_scr.shape, -jnp.inf, F32)

    def score_chunk(c):
        r0 = pl.multiple_of(c * ch, ch)
        bias = bias_scr[c][None]
        for g in range(N_KV_HEADS):
            s = _dot_t(qgs[g], k_scr[pl.ds(r0, ch), gcols[g]])
            s = (s.reshape(HEADS_PER_KV, Q_BLOCK, ch) * scale + bias).reshape(rows, ch)
            s_scr[g, c] = s
            m = m_scr[g]
            for tile in lane_tiles(s):
                m = jnp.maximum(m, tile)
            m_scr[g] = m

    over_chunks(score_chunk)
    for g in range(N_KV_HEADS):
        m_scr[g] = jnp.broadcast_to(jnp.max(m_scr[g], axis=1, keepdims=True), m_scr.shape[1:])
    l_scr[...] = jnp.zeros(l_scr.shape, F32)
    acc_scr[...] = jnp.zeros(acc_scr.shape, F32)

    def value_chunk(c):
        r0 = pl.multiple_of(c * ch, ch)
        for g in range(N_KV_HEADS):
            m = m_scr[g]
            p_tiles = [jnp.exp(tile - m) for tile in lane_tiles(s_scr[g, c])]
            l = l_scr[g]
            for tile in p_tiles:
                l = l + tile
            l_scr[g] = l
            p = jnp.concatenate(p_tiles, axis=1).astype(BF16)
            acc_scr[g] += _dot(p, v_scr[pl.ds(r0, ch), gcols[g]])

    over_chunks(value_chunk)
    for g in range(N_KV_HEADS):
        o = acc_scr[g] / jnp.sum(l_scr[g], axis=1, keepdims=True)
        for r in range(HEADS_PER_KV):
            h = g * HEADS_PER_KV + r
            o_ref[0, :, h * HEAD_DIM:(h + 1) * HEAD_DIM] = o[r * Q_BLOCK:(r + 1) * Q_BLOCK].astype(o_ref.dtype)


def _attn_prompt(q, iq, iwt, kb, vb, ik2, topk):
    b, t_len, _ = q.shape
    n_blk = pl.cdiv(t_len, Q_BLOCK)
    n_chunks = pl.cdiv(n_blk * Q_BLOCK, KEY_CHUNK)
    tk = n_chunks * KEY_CHUNK
    rows = HEADS_PER_KV * Q_BLOCK
    qblk = lambda c: pl.BlockSpec((1, Q_BLOCK, c), lambda bi, i: (bi, i, 0))
    full = lambda c: pl.BlockSpec((1, t_len, c), lambda bi, i: (bi, 0, 0))
    return pl.pallas_call(
        functools.partial(_attn_prompt_kernel, topk=topk),
        grid=(b, n_blk),
        in_specs=[qblk(ATTN_WIDTH), qblk(N_IDX_HEADS * IDX_DIM),
                  pl.BlockSpec((1, N_IDX_HEADS, Q_BLOCK), lambda bi, i: (bi, 0, i)),
                  full(KV_WIDTH), full(KV_WIDTH), full(2 * V7X_LANES)],
        out_specs=qblk(ATTN_WIDTH),
        out_shape=jax.ShapeDtypeStruct((b, t_len, ATTN_WIDTH), BF16),
        scratch_shapes=[pltpu.VMEM((tk, KV_WIDTH), BF16), pltpu.VMEM((tk, KV_WIDTH), BF16),
                        pltpu.VMEM((2, tk, V7X_LANES), BF16),
                        pltpu.VMEM((n_chunks, KEY_CHUNK, Q_BLOCK), jnp.int32),
                        pltpu.VMEM((V7X_SUBLANES_F32, Q_BLOCK), jnp.int32),
                        pltpu.VMEM((V7X_SUBLANES_F32, Q_BLOCK), F32),
                        pltpu.VMEM((n_chunks, Q_BLOCK, KEY_CHUNK), F32),
                        pltpu.VMEM((N_KV_HEADS, n_chunks, rows, KEY_CHUNK), F32),
                        pltpu.VMEM((N_KV_HEADS, rows, V7X_LANES), F32),
                        pltpu.VMEM((N_KV_HEADS, rows, V7X_LANES), F32),
                        pltpu.VMEM((N_KV_HEADS, rows, HEAD_DIM), F32)],
        compiler_params=_params("arbitrary", "arbitrary"),
        name="attn_prompt",
    )(q, iq, iwt, kb, vb, ik2)


def _pool_project(d_parts, pw_ref, ps_ref):
    group = d_parts[0].shape[1]
    outs = []
    for g, d in enumerate(d_parts):
        outs.append(_dot(d.astype(BF16), pw_ref[g]) * ps_ref[:, g * group:(g + 1) * group])
    return jnp.concatenate(outs, axis=1).astype(BF16)


def _pos_in_seq(i, tm, seq_len):
    t = (i * tm) % seq_len + lax.broadcasted_iota(jnp.int32, (tm, 1), 0)
    return jnp.where(t >= seq_len, t - seq_len, t)


def _mix_prompt_kernel(*refs, seq_len, per_seq):
    a_ref, u_ref, wo_ref, pw_ref, ps_ref, g_ref, gnext_ref, y_ref, yn_ref, e_scr = refs[-10:]
    x_refs = refs[:-10]
    i = pl.program_id(0)
    tm = u_ref.shape[0]
    group = pw_ref.shape[1]
    pad = V7X_SUBLANES_F32
    ext = POOL_HALO + tm

    @pl.when(i == 0)
    def _():
        e_scr[0:pad + POOL_HALO, :] = jnp.zeros((pad + POOL_HALO, e_scr.shape[1]), F32)

    u = u_ref[...]
    e_scr[pad + POOL_HALO:pad + ext, :] = u
    t_ext = (i * tm + seq_len - POOL_HALO) % seq_len + lax.broadcasted_iota(jnp.int32, (ext, 1), 0)
    t_ext = jnp.where(t_ext >= seq_len, t_ext - seq_len, t_ext)
    t = t_ext[POOL_HALO:]
    d_parts = []
    for g, w in enumerate(POOL_WINDOWS):
        cols = slice(g * group, (g + 1) * group)
        k = 1
        while True:
            in_seq = (t_ext >= k).astype(F32)
            total = e_scr[pad:pad + ext, cols] + in_seq * e_scr[pad - k:pad - k + ext, cols]
            k *= 2
            if k >= w:
                break
            e_scr[pad:pad + ext, cols] = total
        ug = u[:, cols]
        cnt = jnp.minimum(t + 1, w).astype(F32)
        d_parts.append(total[POOL_HALO:] / cnt - ug)
    e_scr[pad:pad + POOL_HALO, :] = u[tm - POOL_HALO:, :]
    p = _pool_project(d_parts, pw_ref, ps_ref)
    o = _dot(a_ref[...], wo_ref[0:ATTN_WIDTH, :]) + _dot(p, wo_ref[ATTN_WIDTH:, :])
    y = _stream_tile(x_refs, per_seq) + _rms(o, g_ref[...])
    y_ref[...] = y
    yn_ref[...] = _rms(y, gnext_ref[...]).astype(BF16)


def _mix_prompt(x, a, u, wo, pw, ps, g, g_next, tm, seq_len, meta=None):
    x_specs, x_ops, per_seq = _stream_specs(x, tm, meta)
    d = x.shape[-1]
    m, pool_width = u.shape
    assert POOL_HALO <= tm <= seq_len
    assert all(w >= 2 and w & (w - 1) == 0 and w // 2 <= V7X_SUBLANES_F32 for w in POOL_WINDOWS)
    row = lambda c: pl.BlockSpec((tm, c), lambda i: (i, 0))
    return pl.pallas_call(
        functools.partial(_mix_prompt_kernel, seq_len=seq_len, per_seq=per_seq),
        grid=(m // tm,),
        in_specs=x_specs + [row(ATTN_WIDTH), row(pool_width), _const_spec(wo.shape), _const_spec(pw.shape),
                            _const_spec((1, pool_width)), _const_spec((1, d)), _const_spec((1, d))],
        out_specs=[row(d), row(d)],
        out_shape=[jax.ShapeDtypeStruct((m, d), F32), jax.ShapeDtypeStruct((m, d), BF16)],
        scratch_shapes=[pltpu.VMEM((V7X_SUBLANES_F32 + POOL_HALO + tm, pool_width), F32)],
        compiler_params=_params("arbitrary"),
        name="mix_prompt",
    )(*x_ops, a, u, wo, pw, ps, g, g_next)


def _ffn_up_kernel(xn_ref, wg_ref, wv_ref, cwg_ref, cwv_ref, cbg_ref, cbv_ref,
                   act_ref, lastg_ref, lastv_ref, hg_scr, hv_scr, carryg_scr, carryv_scr, *, seq_len):
    i = pl.program_id(1)
    tm = xn_ref.shape[0]

    @pl.when(i == 0)
    def _():
        carryg_scr[...] = jnp.zeros(carryg_scr.shape, F32)
        carryv_scr[...] = jnp.zeros(carryv_scr.shape, F32)

    t = _pos_in_seq(i, tm, seq_len)
    in_seq1 = (t >= 1).astype(F32)
    in_seq2 = (t >= 2).astype(F32)
    seq_end = jnp.minimum(seq_len - (i * tm) % seq_len, tm)
    last_row0 = pl.multiple_of(seq_end, CONV_HALO)

    xn = xn_ref[...]
    acts = []
    for w_ref, cw_ref, cb_ref, h_scr, carry_scr, last_ref in (
            (wg_ref, cwg_ref, cbg_ref, hg_scr, carryg_scr, lastg_ref),
            (wv_ref, cwv_ref, cbv_ref, hv_scr, carryv_scr, lastv_ref)):
        h = _dot(xn, w_ref[...])
        h_scr[0:CONV_HALO, :] = carry_scr[...]
        h_scr[CONV_HALO:CONV_HALO + tm, :] = h
        carry_scr[...] = h[tm - CONV_HALO:, :]
        last_ref[0] = h_scr[pl.ds(last_row0, CONV_HALO), :]
        acts.append(cb_ref[...]
                    + cw_ref[0:1, :] * (in_seq2 * h_scr[CONV_HALO - 2:CONV_HALO - 2 + tm, :])
                    + cw_ref[1:2, :] * (in_seq1 * h_scr[CONV_HALO - 1:CONV_HALO - 1 + tm, :])
                    + cw_ref[2:3, :] * h)
    cg, cv = acts
    act_ref[...] = (cg * jax.nn.sigmoid(cg) * cv).astype(BF16)


def _ffn_up(xn, wup, cw, cb, layer, tm, tf, seq_len):
    m, d = xn.shape
    d_ff = wup.shape[2] // 2
    nj = d_ff // tf
    n_tiles = m // tm
    assert CONV_HALO <= tm <= seq_len and seq_len % CONV_HALO == 0 and tm % CONV_HALO == 0
    return pl.pallas_call(
        functools.partial(_ffn_up_kernel, seq_len=seq_len),
        grid=(nj, n_tiles),
        in_specs=[
            pl.BlockSpec((tm, d), lambda j, i: (i, 0)),
            pl.BlockSpec((None, d, tf), lambda j, i: (layer, 0, j)),
            pl.BlockSpec((None, d, tf), lambda j, i: (layer, 0, j + nj)),
            pl.BlockSpec((CONV_WIDTH, tf), lambda j, i: (0, j)),
            pl.BlockSpec((CONV_WIDTH, tf), lambda j, i: (0, j + nj)),
            pl.BlockSpec((1, tf), lambda j, i: (0, j)),
            pl.BlockSpec((1, tf), lambda j, i: (0, j + nj)),
        ],
        out_specs=[
            pl.BlockSpec((tm, tf), lambda j, i: (i, j)),
            pl.BlockSpec((1, CONV_HALO, tf), lambda j, i: (i, 0, j)),
            pl.BlockSpec((1, CONV_HALO, tf), lambda j, i: (i, 0, j)),
        ],
        out_shape=[jax.ShapeDtypeStruct((m, d_ff), BF16),
                   jax.ShapeDtypeStruct((n_tiles, CONV_HALO, d_ff), F32),
                   jax.ShapeDtypeStruct((n_tiles, CONV_HALO, d_ff), F32)],
        scratch_shapes=[pltpu.VMEM((CONV_HALO + tm, tf), F32), pltpu.VMEM((CONV_HALO + tm, tf), F32),
                        pltpu.VMEM((CONV_HALO, tf), F32), pltpu.VMEM((CONV_HALO, tf), F32)],
        compiler_params=_params("arbitrary", "arbitrary"),
        name="ffn_up",
    )(xn, wup, wup, cw, cw, cb, cb)


def _ffn_down_kernel(act_ref, x_ref, wd_ref, g_ref, y_ref):
    y_ref[...] = x_ref[...] + _rms(_dot(act_ref[...], wd_ref[...]), g_ref[...])


def _ffn_down_tail(act, x, wd, g, layer, n_seq, n_skip, tile):
    m, d = x.shape
    d_ff = act.shape[1]
    seq_len = m // n_seq
    rows = seq_len - n_skip
    assert rows % tile == 0 and n_skip % V7X_SUBLANES_BF16 == 0
    seq_rows = lambda c: pl.BlockSpec((pl.Element(1), pl.Element(tile), pl.Element(c)),
                                      lambda s, k: (s, pl.multiple_of(n_skip + k * tile, V7X_SUBLANES_BF16), 0))

    def body(act_ref, x_ref, wd_ref, g_ref, y_ref):
        _ffn_down_kernel(act_ref.at[0], x_ref.at[0], wd_ref, g_ref, y_ref)

    return pl.pallas_call(
        body,
        grid=(n_seq, rows // tile),
        in_specs=[seq_rows(d_ff), seq_rows(d),
                  pl.BlockSpec((None, d_ff, d), lambda s, k: (layer, 0, 0), pipeline_mode=pl.Buffered(1)),
                  _const_spec((1, d))],
        out_specs=pl.BlockSpec((None, tile, d), lambda s, k: (s, k, 0)),
        out_shape=jax.ShapeDtypeStruct((n_seq, rows, d), F32),
        compiler_params=_params("arbitrary", "arbitrary"),
        name="ffn_down_tail",
    )(act.reshape(n_seq, seq_len, d_ff), x.reshape(n_seq, seq_len, d), wd, g)


def _ffn_down(act, x, wd, g, layer, tm):
    m, d = x.shape
    d_ff = act.shape[1]
    row = lambda c: pl.BlockSpec((tm, c), lambda i: (i, 0))
    return pl.pallas_call(
        _ffn_down_kernel,
        grid=(m // tm,),
        in_specs=[row(d_ff), row(d),
                  pl.BlockSpec((None, d_ff, d), lambda i: (layer, 0, 0), pipeline_mode=pl.Buffered(1)),
                  _const_spec((1, d))],
        out_specs=row(d),
        out_shape=jax.ShapeDtypeStruct((m, d), F32),
        compiler_params=_params("arbitrary"),
        name="ffn_down",
    )(act, x, wd, g)


def _idx_sample_kernel(pt_ref, iq_ref, iw_ref, new_ref, *refs):
    del pt_ref
    page_refs, o_ref, onew_ref = refs[:-2], refs[-2], refs[-1]
    iq = iq_ref[0]
    iw = iw_ref[0] * (IDX_DIM ** -0.5 * N_IDX_HEADS ** -0.5)

    def scores(keys_t):
        s = _dot(iq, keys_t.astype(BF16))
        return jnp.sum(jnp.maximum(s, 0.0) * iw, axis=0, keepdims=True)

    o_ref[0] = scores(jnp.concatenate([r[0, 0] for r in page_refs], axis=1))
    onew_ref[0] = scores(new_ref[0])


def _page_specs(layer, block, n_pages, per_step):
    def spec(p):
        def index(b, c, pt):
            return (layer, pt[b * n_pages + c * per_step + p]) + (0,) * (len(block) - 2)
        return pl.BlockSpec(block, index)
    return [spec(p) for p in range(per_step)]


def _idx_sample(pt_flat, iq3, iw3, new_page, cache_kidx, layer, n_pages):
    db = iq3.shape[0]
    per_step = min(n_pages, IDX_PAGES_PER_STEP)
    assert n_pages % per_step == 0
    grid_spec = pltpu.PrefetchScalarGridSpec(
        num_scalar_prefetch=1,
        grid=(db, n_pages // per_step),
        in_specs=[pl.BlockSpec((1, N_IDX_HEADS, IDX_DIM), lambda b, c, pt: (b, 0, 0)),
                  pl.BlockSpec((1, N_IDX_HEADS, 1), lambda b, c, pt: (b, 0, 0)),
                  pl.BlockSpec((1, IDX_DIM, PAGE_SIZE), lambda b, c, pt: (b, 0, 0))]
                 + _page_specs(layer, (1, 1, IDX_DIM, PAGE_SIZE), n_pages, per_step),
        out_specs=[pl.BlockSpec((1, 1, per_step * PAGE_SIZE), lambda b, c, pt: (b, 0, c)),
                   pl.BlockSpec((1, 1, PAGE_SIZE), lambda b, c, pt: (b, 0, 0))],
    )
    return pl.pallas_call(
        _idx_sample_kernel,
        grid_spec=grid_spec,
        out_shape=[jax.ShapeDtypeStruct((db, 1, n_pages * PAGE_SIZE), F32),
                   jax.ShapeDtypeStruct((db, 1, PAGE_SIZE), F32)],
        compiler_params=_params("arbitrary", "arbitrary"),
        name="idx_sample",
    )(pt_flat, iq3, iw3, new_page, *([cache_kidx] * per_step))


def _select_sample_kernel(sc_ref, scn_ref, bias_ref, biasn_ref, key_scr, *, topk):
    db = sc_ref.shape[0]
    key_new = _sort_key(scn_ref[:, 0:1])
    key_scr[...] = _sort_key(sc_ref[...])

    def count_ge(t):
        past = jnp.sum(jnp.where(key_scr[...] >= t, 1.0, 0.0), axis=1, keepdims=True)
        return past + jnp.where(key_new >= t, 1.0, 0.0)

    thr, cnt_t = _kth_largest_key(count_ge, (db, 1), topk)
    tied = jnp.max(jnp.where(cnt_t > topk, 1.0, 0.0)) > 0.0

    @pl.when(jnp.logical_not(tied))
    def _():
        bias_ref[...] = jnp.where(key_scr[...] >= thr, 0.0, -jnp.inf)
        biasn_ref[...] = jnp.where(key_new >= thr, 0.0, -jnp.inf)

    @pl.when(tied)
    def _():
        above = (jnp.sum(jnp.where(key_scr[...] > thr, 1.0, 0.0), axis=1, keepdims=True)
                 + jnp.where(key_new > thr, 1.0, 0.0))
        room = topk - above
        rank = _tie_rank_matrix(V7X_LANES, lower=False)
        seen = jnp.zeros((db, 1), F32)
        for blk in range(sc_ref.shape[1] // V7X_LANES):
            cols = slice(blk * V7X_LANES, (blk + 1) * V7X_LANES)
            key = key_scr[:, cols]
            eq = jnp.where(key == thr, 1.0, 0.0)
            pos = _dot(eq.astype(BF16), rank) + seen
            keep = jnp.where(key > thr, 1.0, jnp.where(pos <= room, eq, 0.0))
            bias_ref[:, cols] = jnp.where(keep > 0.0, 0.0, -jnp.inf)
            seen = seen + jnp.sum(eq, axis=1, keepdims=True)
        keep_new = (key_new > thr) | ((key_new == thr) & (seen + 1.0 <= room))
        biasn_ref[...] = jnp.where(keep_new, 0.0, -jnp.inf)


def _select_sample(scores, scores_new, topk):
    db, past = scores.shape
    return pl.pallas_call(
        functools.partial(_select_sample_kernel, topk=topk),
        out_shape=[jax.ShapeDtypeStruct((db, past), F32), jax.ShapeDtypeStruct((db, 1), F32)],
        scratch_shapes=[pltpu.VMEM((db, past), jnp.int32)],
        compiler_params=pltpu.CompilerParams(vmem_limit_bytes=V7X_VMEM_LIMIT_BYTES),
        name="select_sample",
    )(scores, scores_new)


def _attn_sample_kernel(pt_ref, q_ref, bias_ref, biasn_ref, kn_ref, vn_ref, *refs):
    del pt_ref
    per_step = (len(refs) - 4) // 2
    k_refs = refs[:per_step]
    v_refs = refs[per_step:2 * per_step]
    o_ref, m_scr, l_scr, acc_scr = refs[2 * per_step:]
    c = pl.program_id(1)
    scale = HEAD_DIM ** -0.5

    @pl.when(c == 0)
    def _():
        m_scr[...] = jnp.full(m_scr.shape, -jnp.inf, F32)
        l_scr[...] = jnp.zeros(l_scr.shape, F32)
        acc_scr[...] = jnp.zeros(acc_scr.shape, F32)

    def update(g, s, v):
        m_old = m_scr[g]
        m_new = jnp.maximum(m_old, jnp.max(s, axis=-1, keepdims=True))
        safe = jnp.where(m_new == -jnp.inf, 0.0, m_new)
        alpha = jnp.exp(m_old - safe)
        p = jnp.exp(s - safe)
        l_scr[g] = alpha * l_scr[g] + jnp.sum(p, axis=-1, keepdims=True)
        acc_scr[g] = alpha * acc_scr[g] + _dot(p.astype(BF16), v)
        m_scr[g] = m_new

    bias = bias_ref[0]
    for g in range(N_KV_HEADS):
        kg = jnp.concatenate([r[0, 0, pl.ds(g, PAGE_SIZE, stride=N_KV_HEADS), :] for r in k_refs], axis=0)
        vg = jnp.concatenate([r[0, 0, pl.ds(g, PAGE_SIZE, stride=N_KV_HEADS), :] for r in v_refs], axis=0)
        s = _dot_t(q_ref[0, g], kg.astype(BF16)) * scale + bias
        update(g, s, vg.astype(BF16))

    @pl.when(c == pl.num_programs(1) - 1)
    def _():
        for g in range(N_KV_HEADS):
            kn = kn_ref[0, :, g * HEAD_DIM:(g + 1) * HEAD_DIM].astype(BF16)
            vn = vn_ref[0, :, g * HEAD_DIM:(g + 1) * HEAD_DIM].astype(BF16)
            qf = q_ref[0, g].astype(F32)
            s = jnp.sum(qf * kn.astype(F32), axis=-1, keepdims=True) * scale + biasn_ref[0]
            m_old = m_scr[g]
            m_new = jnp.maximum(m_old, s)
            alpha = jnp.exp(m_old - m_new)
            p = jnp.exp(s - m_new)
            l = alpha * l_scr[g] + p
            acc = alpha * acc_scr[g] + p.astype(BF16).astype(F32) * vn.astype(F32)
            o_ref[0, g] = acc / l


def _attn_sample(pt_flat, q4, bias3, bias_new, k_new, v_new, cache_k, cache_v, layer, n_pages):
    db, _, rows, _ = q4.shape
    per_step = min(n_pages, ATTN_PAGES_PER_STEP)
    assert n_pages % per_step == 0
    page_block = (1, 1, PAGE_SIZE * N_KV_HEADS, HEAD_DIM)
    pages = _page_specs(layer, page_block, n_pages, per_step)
    grid_spec = pltpu.PrefetchScalarGridSpec(
        num_scalar_prefetch=1,
        grid=(db, n_pages // per_step),
        in_specs=[pl.BlockSpec((1, N_KV_HEADS, rows, HEAD_DIM), lambda b, c, pt: (b, 0, 0, 0)),
                  pl.BlockSpec((1, 1, per_step * PAGE_SIZE), lambda b, c, pt: (b, 0, c)),
                  pl.BlockSpec((1, 1, 1), lambda b, c, pt: (b, 0, 0)),
                  pl.BlockSpec((1, 1, KV_WIDTH), lambda b, c, pt: (b, 0, 0)),
                  pl.BlockSpec((1, 1, KV_WIDTH), lambda b, c, pt: (b, 0, 0))] + pages + pages,
        out_specs=pl.BlockSpec((1, N_KV_HEADS, rows, HEAD_DIM), lambda b, c, pt: (b, 0, 0, 0)),
        scratch_shapes=[pltpu.VMEM((N_KV_HEADS, rows, 1), F32),
                        pltpu.VMEM((N_KV_HEADS, rows, 1), F32),
                        pltpu.VMEM((N_KV_HEADS, rows, HEAD_DIM), F32)],
    )
    return pl.pallas_call(
        _attn_sample_kernel,
        grid_spec=grid_spec,
        out_shape=jax.ShapeDtypeStruct((db, N_KV_HEADS, rows, HEAD_DIM), F32),
        compiler_params=_params("arbitrary", "arbitrary"),
        name="attn_sample",
    )(pt_flat, q4, bias3, bias_new, k_new, v_new, *([cache_k] * per_step), *([cache_v] * per_step))


def _mix_sample_kernel(x_ref, a_ref, e_ref, wo_ref, pw_ref, ps_ref, g_ref, y_ref):
    group = pw_ref.shape[1]
    n_rows = e_ref.shape[0]
    d_parts = []
    for g, w in enumerate(POOL_WINDOWS):
        cols = slice(g * group, (g + 1) * group)
        ug = e_ref[n_rows - 1, :, cols]
        wsum = ug
        for s in range(1, w):
            wsum = wsum + e_ref[n_rows - 1 - s, :, cols]
        d_parts.append(wsum / float(w) - ug)
    p = _pool_project(d_parts, pw_ref, ps_ref)
    o = _dot(a_ref[...].astype(BF16), wo_ref[0:ATTN_WIDTH, :]) + _dot(p, wo_ref[ATTN_WIDTH:, :])
    y_ref[...] = x_ref[...] + _rms(o, g_ref[...])


def _mix_sample(x, a, e, wo, pw, ps, g):
    return pl.pallas_call(
        _mix_sample_kernel,
        out_shape=jax.ShapeDtypeStruct(x.shape, F32),
        compiler_params=pltpu.CompilerParams(vmem_limit_bytes=V7X_VMEM_LIMIT_BYTES),
        name="mix_sample",
    )(x, a, e, wo, pw, ps, g)


def _ffn_sample_kernel(x_ref, gpre_ref, wg_ref, wv_ref, cwg_ref, cwv_ref, cbg_ref, cbv_ref, sg_ref, sv_ref,
                       wd_ref, gpost_ref, y_ref, hg_ref, hv_ref, xn_scr, acc_scr):
    j = pl.program_id(0)

    @pl.when(j == 0)
    def _():
        xn_scr[...] = _rms(x_ref[...], gpre_ref[...]).astype(BF16)
        acc_scr[...] = jnp.zeros(acc_scr.shape, F32)

    xn = xn_scr[...]
    hg = _dot(xn, wg_ref[...])
    hv = _dot(xn, wv_ref[...])
    hg_ref[...] = hg
    hv_ref[...] = hv

    def conv(h, s_ref, cw_ref, cb_ref):
        return cb_ref[...] + cw_ref[0:1, :] * s_ref[0] + cw_ref[1:2, :] * s_ref[1] + cw_ref[2:3, :] * h

    cg = conv(hg, sg_ref, cwg_ref, cbg_ref)
    cv = conv(hv, sv_ref, cwv_ref, cbv_ref)
    act = (cg * jax.nn.sigmoid(cg) * cv).astype(BF16)
    acc_scr[...] += _dot(act, wd_ref[...])

    @pl.when(j == pl.num_programs(0) - 1)
    def _():
        y_ref[...] = x_ref[...] + _rms(acc_scr[...], gpost_ref[...])


def _ffn_sample(x, gpre, wup, cw, cb, state, wd, gpost, layer, tf):
    db, d = x.shape
    d_ff = wd.shape[1]
    nj = d_ff // tf
    n_state = state.shape[0]
    return pl.pallas_call(
        _ffn_sample_kernel,
        grid=(nj,),
        in_specs=[
            pl.BlockSpec((db, d), lambda j: (0, 0)),
            pl.BlockSpec((1, d), lambda j: (0, 0)),
            pl.BlockSpec((None, d, tf), lambda j: (layer, 0, j)),
            pl.BlockSpec((None, d, tf), lambda j: (layer, 0, j + nj)),
            pl.BlockSpec((CONV_WIDTH, tf), lambda j: (0, j)),
            pl.BlockSpec((CONV_WIDTH, tf), lambda j: (0, j + nj)),
            pl.BlockSpec((1, tf), lambda j: (0, j)),
            pl.BlockSpec((1, tf), lambda j: (0, j + nj)),
            pl.BlockSpec((n_state, db, tf), lambda j: (0, 0, j)),
            pl.BlockSpec((n_state, db, tf), lambda j: (0, 0, j + nj)),
            pl.BlockSpec((None, tf, d), lambda j: (layer, j, 0)),
            pl.BlockSpec((1, d), lambda j: (0, 0)),
        ],
        out_specs=[pl.BlockSpec((db, d), lambda j: (0, 0)),
                   pl.BlockSpec((db, tf), lambda j: (0, j)),
                   pl.BlockSpec((db, tf), lambda j: (0, j))],
        out_shape=[jax.ShapeDtypeStruct((db, d), F32), jax.ShapeDtypeStruct((db, d_ff), F32),
                   jax.ShapeDtypeStruct((db, d_ff), F32)],
        scratch_shapes=[pltpu.VMEM((db, d), BF16), pltpu.VMEM((db, d), F32)],
        compiler_params=_params("arbitrary"),
        name="ffn_sample",
    )(x, gpre, wup, wup, cw, cw, cb, cb, state, state, wd, gpost)


def _split_w_in(w_in_l):
    d = w_in_l.shape[0]
    q_end = ATTN_WIDTH
    k_end = q_end + KV_WIDTH
    v_end = k_end + KV_WIDTH
    iq_end = v_end + N_IDX_HEADS * IDX_DIM
    ik_end = iq_end + IDX_DIM
    iw_end = ik_end + N_IDX_HEADS
    w = w_in_l.astype(BF16)
    w_ik = w[:, iq_end:ik_end]
    zeros = lambda c: jnp.zeros((d, c), BF16)
    idx = jnp.concatenate([w_ik, zeros(V7X_LANES - IDX_DIM), zeros(V7X_LANES - IDX_DIM), w_ik,
                           w[:, ik_end:iw_end], zeros(V7X_LANES - N_IDX_HEADS)], axis=1)
    return {"q": w[:, :q_end], "kv": w[:, q_end:v_end], "iq": w[:, v_end:iq_end], "idx": idx, "u": w[:, iw_end:]}


def kernel(x_prompt, x_sample, cache_k, cache_v, cache_kidx, state_pool, state_conv, page_table, meta_tokens,
           w_in, w_o, pool_w, pool_scale, g_mix_pre, g_mix_post, g_ffn_pre, g_ffn_post, w_up, conv_w, conv_b,
           w_down):
    b, seq, d = x_prompt.shape
    db, ds, _ = x_sample.shape
    assert ds == 1, "the sample kernels handle one new token per sequence"
    depth = w_in.shape[0]
    n_pool = cache_k.shape[1]
    n_pages = page_table.shape[1]
    assert cache_k.shape[2] == PAGE_SIZE
    past = n_pages * PAGE_SIZE
    t_len = seq + N_META
    d_ff = w_down.shape[1]
    pool_width = d - ATTN_WIDTH
    topk_p = min(TOPK_MAX, t_len // 4)
    topk_s = min(TOPK_MAX, (past + ds) // 4)
    assert past >= max(POOL_WINDOWS)

    m_rows = b * t_len
    tm_proj = _row_tile(m_rows, min(TM_PROJ_CAP, t_len))
    tm_ffn = _row_tile(m_rows, min(TM_FFN_CAP, t_len))
    tm_down = _row_tile(m_rows, min(TM_DOWN_CAP, t_len))
    tf = _ff_tile(d_ff)
    sub8 = V7X_SUBLANES_F32
    heads_pad = -(-HEADS_PER_KV // sub8) * sub8

    tm_first = _row_tile(t_len, TM_FFN_CAP)
    meta = meta_tokens.astype(x_prompt.dtype)
    first_in_place = tm_first > N_META
    if first_in_place:
        xp = x_prompt
    else:
        xp = jnp.concatenate([jnp.broadcast_to(meta[None], (b, N_META, d)), x_prompt], axis=1).reshape(m_rows, d)
    xs = x_sample.reshape(db, d)
    pt_flat = page_table.reshape(-1).astype(jnp.int32)
    ck = cache_k.reshape(depth, n_pool, PAGE_SIZE * N_KV_HEADS, HEAD_DIM)
    cv = cache_v.reshape(depth, n_pool, PAGE_SIZE * N_KV_HEADS, HEAD_DIM)
    kidx_t = jnp.swapaxes(cache_kidx, 2, 3)

    row2 = lambda a: a.reshape(1, -1)
    outs = {name: [] for name in ("kp", "vp", "ikp", "poolp", "convp", "ks", "vs", "iks", "pools", "convs")}
    wup = w_up.astype(BF16)
    wd = w_down.astype(BF16)
    for l in range(depth):
        w = _split_w_in(w_in[l])
        wo = w_o[l].astype(BF16)
        pw = pool_w[l].astype(BF16)
        ps = row2(pool_scale[l])
        cw = conv_w[l]
        cb = row2(conv_b[l])

        stream = dict(tm=tm_first, meta=meta) if (l == 0 and first_in_place) else dict(tm=tm_proj)
        q, k, v, kb, vb, iq, ik, ik2, iw, u = _in_proj(xp, row2(g_mix_pre[l]), w, **stream)
        b3 = lambda a: a.reshape(b, t_len, a.shape[-1])
        a = _attn_prompt(b3(q), b3(iq), jnp.swapaxes(b3(iw), 1, 2), b3(kb), b3(vb), b3(ik2), topk_p)
        xp, xn = _mix_prompt(xp, a.reshape(m_rows, ATTN_WIDTH), u, wo, pw, ps, row2(g_mix_post[l]),
                             row2(g_ffn_pre[l]), seq_len=t_len, **stream)
        act, last_g, last_v = _ffn_up(xn, wup, cw, cb, l, tm_ffn, tf, t_len)
        if l < depth - 1:
            xp = _ffn_down(act, xp, wd, row2(g_ffn_post[l]), l, tm_down)
        else:
            y_prompt = _ffn_down_tail(act, xp, wd, row2(g_ffn_post[l]), l, b, N_META, _row_tile(seq, TM_TAIL_CAP))
        outs["kp"].append(k.reshape(b, t_len, N_KV_HEADS, HEAD_DIM))
        outs["vp"].append(v.reshape(b, t_len, N_KV_HEADS, HEAD_DIM))
        outs["ikp"].append(ik.reshape(b, t_len, IDX_DIM))
        outs["poolp"].append(u.reshape(b, t_len, pool_width)[:, t_len - POOL_BUF:])
        tail = slice(CONV_HALO - (CONV_WIDTH - 1), CONV_HALO)
        if t_len % tm_ffn == 0:
            per_seq = t_len // tm_ffn
            seq_last = lambda h: h[per_seq - 1::per_seq, tail]
        else:
            end_tiles = [((s + 1) * t_len - 1) // tm_ffn for s in range(b)]
            seq_last = lambda h: jnp.stack([h[i, tail] for i in end_tiles])
        outs["convp"].append(jnp.concatenate([seq_last(last_g), seq_last(last_v)], axis=-1))

        q, k, v, _, _, iq, ik, _, iw, u = _in_proj(xs, row2(g_mix_pre[l]), w, db)
        iq3 = iq.reshape(db, N_IDX_HEADS, IDX_DIM)
        new_page = jnp.pad(ik[:, :, None], ((0, 0), (0, 0), (0, PAGE_SIZE - 1)))
        scores, scores_new = _idx_sample(pt_flat, iq3, iw.reshape(db, N_IDX_HEADS, 1), new_page, kidx_t, l, n_pages)
        bias, bias_new = _select_sample(scores.reshape(db, past), scores_new.reshape(db, PAGE_SIZE), topk_s)
        q4 = jnp.pad(q.reshape(db, N_KV_HEADS, HEADS_PER_KV, HEAD_DIM),
                     ((0, 0), (0, 0), (0, heads_pad - HEADS_PER_KV), (0, 0)))
        a = _attn_sample(pt_flat, q4, bias.reshape(db, 1, past), bias_new.reshape(db, 1, 1),
                         k.reshape(db, 1, KV_WIDTH), v.reshape(db, 1, KV_WIDTH), ck, cv, l, n_pages)
        a = a[:, :, :HEADS_PER_KV].reshape(db, ATTN_WIDTH)
        e = jnp.concatenate([jnp.swapaxes(state_pool[l], 0, 1).astype(u.dtype), u[None]], axis=0)
        xs = _mix_sample(xs, a, e, wo, pw, ps, row2(g_mix_post[l]))
        xs, hg, hv = _ffn_sample(xs, row2(g_ffn_pre[l]), wup, cw, cb, jnp.swapaxes(state_conv[l], 0, 1), wd,
                                 row2(g_ffn_post[l]), l, tf)
        outs["ks"].append(k.reshape(db, ds, N_KV_HEADS, HEAD_DIM))
        outs["vs"].append(v.reshape(db, ds, N_KV_HEADS, HEAD_DIM))
        outs["iks"].append(ik.reshape(db, ds, IDX_DIM))
        outs["pools"].append(jnp.concatenate([state_pool[l].astype(u.dtype), u[:, None, :]], axis=1)[:, 1:])
        h = jnp.concatenate([hg, hv], axis=-1)[:, None, :]
        outs["convs"].append(jnp.concatenate([state_conv[l].astype(h.dtype), h], axis=1)[:, -(CONV_WIDTH - 1):])

    y_sample = xs.reshape(db, ds, d)
    stack = lambda name: jnp.stack(outs[name])
    return (y_prompt, y_sample, stack("kp"), stack("vp"), stack("ikp"), stack("poolp"), stack("convp"),
            stack("ks"), stack("vs"), stack("iks"), stack("pools"), stack("convs"))
```
